```python
import math
import jax
import jax.numpy as jnp
from jax import lax
import numpy as np

D_MODEL = 1024
BATCH = 8
SEQ = 4096
DEPTH = 2
DEC_BATCH = 16
DEC_SEQ = 64
PAST_LEN = 4096

CHUNK = 64
EPS = 1e-6
ROPE_THETA = 10000.0
RET_HEADS = 4
RET_DK = 64
RET_DV = 128
SWA_HEADS = 8
SWA_KV_HEADS = 2
SWA_DH = 64
WINDOW = 128
WINDOW_CHUNKS = WINDOW // CHUNK
SSM_WIDTH = D_MODEL // 2
SSM_GROUP = 16
SSM_GROUPS = SSM_WIDTH // SSM_GROUP
SSM_STATE = 64
CONV_WIDTH = D_MODEL // 2
CONV_K = 31
D_FF = 2816
N_EXPERTS = 8
TOP_K = 2
D_FF_EXPERT = 3584
IN0_SIZES = (RET_HEADS * RET_DK, RET_HEADS * RET_DK, RET_HEADS * RET_DV, RET_HEADS * RET_DV,
             SWA_HEADS * SWA_DH, SWA_KV_HEADS * SWA_DH, SWA_KV_HEADS * SWA_DH)
IN0_COLS = 2 * RET_HEADS * RET_DK + 2 * RET_HEADS * RET_DV + SWA_HEADS * SWA_DH + 2 * SWA_KV_HEADS * SWA_DH
OUT0_COLS = RET_HEADS * RET_DV + SWA_HEADS * SWA_DH
IN1_COLS = SSM_WIDTH + 2 * CONV_WIDTH
OUT1_COLS = SSM_WIDTH + CONV_WIDTH

kernel_name = 'hybrid_chunk_streaming_encoder_step'


def _offsets(sizes):
    out, acc = [], 0
    for s in sizes[:-1]:
        acc += s
        out.append(acc)
    return out


def rms_norm(x, g):
    xf = x.astype(jnp.float32)
    y = xf * lax.rsqrt(jnp.mean(xf * xf, axis=-1, keepdims=True) + EPS)
    return (y * g.astype(jnp.float32)).astype(x.dtype)


def layer_norm(x, g, b):
    xf = x.astype(jnp.float32)
    xc = xf - jnp.mean(xf, axis=-1, keepdims=True)
    var = jnp.mean(xc * xc, axis=-1, keepdims=True)
    return (xc * lax.rsqrt(var + EPS) * g.astype(jnp.float32) + b.astype(jnp.float32)).astype(x.dtype)


def head_group_norm(o, g):
    of = o.astype(jnp.float32)
    xc = of - jnp.mean(of, axis=-1, keepdims=True)
    y = xc * lax.rsqrt(jnp.mean(xc * xc, axis=-1, keepdims=True) + EPS)
    b, l, h, dv = o.shape
    return y.reshape(b, l, h * dv) * g.astype(jnp.float32)


def rope(x, pos):
    half = x.shape[-1] // 2
    inv = jnp.exp(-math.log(ROPE_THETA) * jnp.arange(half, dtype=jnp.float32) / half)
    ang = pos.astype(jnp.float32)[:, None] * inv[None, :]
    cos = jnp.cos(ang)[None, :, None, :]
    sin = jnp.sin(ang)[None, :, None, :]
    xf = x.astype(jnp.float32)
    x1, x2 = xf[..., :half], xf[..., half:]
    return jnp.concatenate([x1 * cos - x2 * sin, x2 * cos + x1 * sin], axis=-1).astype(x.dtype)


def retention(q, k, v, s0):
    b, l, h, dk = q.shape
    dv = v.shape[-1]
    c = min(CHUNK, l)
    n = l // c
    log_g = jnp.log1p(-jnp.exp2(-5.0 - jnp.arange(h, dtype=jnp.float32)))
    idx = jnp.arange(c, dtype=jnp.float32)
    intra = jnp.exp(jnp.abs(idx[:, None] - idx[None, :])[None] * log_g[:, None, None])
    q_dec = jnp.exp((idx + 1.0)[:, None] * log_g[None, :])
    k_dec = jnp.exp((c - 1.0 - idx)[:, None] * log_g[None, :])
    chunk_dec = jnp.exp(c * log_g)
    qc = q.astype(jnp.float32).reshape(b, n, c, h, dk)
    kc = k.astype(jnp.float32).reshape(b, n, c, h, dk)
    vc = v.astype(jnp.float32).reshape(b, n, c, h, dv)
    scores = jnp.einsum('bnihd,bnjhd->bnhij', qc, kc) * intra
    o_intra = jnp.einsum('bnhij,bnjhe->bnihe', scores, vc)
    kv = jnp.einsum('bnjhd,bnjhe->bnhde', kc * k_dec[None, None, :, :, None], vc)

    def step(s, kv_c):
        return chunk_dec[None, :, None, None] * s + kv_c, s

    s_final, s_before = lax.scan(step, s0.astype(jnp.float32), jnp.moveaxis(kv, 1, 0))
    s_before = jnp.moveaxis(s_before, 0, 1)
    o_cross = jnp.einsum('bnihd,bnhde->bnihe', qc * q_dec[None, None, :, :, None], s_before)
    return (o_intra + o_cross).reshape(b, l, h, dv), s_final


def sink_attention(q, k, v, mask, sinks):
    b, n, nq, hq, dh = q.shape
    hkv = k.shape[3]
    grp = hq // hkv
    qg = q.reshape(b, n, nq, hkv, grp, dh)
    s = jnp.einsum('bnqhgd,bnkhd->bnhgqk', qg, k).astype(jnp.float32) * (dh ** -0.5)
    if mask is not None:
        s = jnp.where(mask, s, -1e30)
    sink = sinks.astype(jnp.float32).reshape(1, 1, hkv, grp, 1, 1)
    m = jnp.maximum(jnp.max(s, axis=-1, keepdims=True), sink)
    p = jnp.exp(s - m)
    p = p / (jnp.sum(p, axis=-1, keepdims=True) + jnp.exp(sink - m))
    o = jnp.einsum('bnhgqk,bnkhd->bnqhgd', p.astype(v.dtype), v)
    return o.reshape(b, n, nq, hq * dh)


def swiglu(x, wg, wu, wd):
    return (jax.nn.silu(x @ wg) * (x @ wu)) @ wd


def moe_swiglu(x, router, wg, wu, wd):
    logits = (x @ router).astype(jnp.float32)
    top_v, top_i = lax.top_k(logits, TOP_K)
    gates = jax.nn.softmax(top_v, axis=-1)
    dense_gate = jnp.sum(jax.nn.one_hot(top_i, N_EXPERTS, dtype=jnp.float32) * gates[..., None], axis=-2)
    out = jnp.zeros(x.shape, jnp.float32)
    for e in range(N_EXPERTS):
        out = out + dense_gate[..., e:e + 1] * swiglu(x, wg[e], wu[e], wd[e]).astype(jnp.float32)
    return out


def even_mixer(h, pos, ret_s0, swa_past, w_in0, ret_gn, swa_qnorm, swa_knorm, swa_sinks, w_out0):
    b, l, _ = h.shape
    z = h @ w_in0
    qa, ka, va, ga, qb, kb, vb = jnp.split(z, _offsets(IN0_SIZES), axis=-1)
    qa = rope(qa.reshape(b, l, RET_HEADS, RET_DK), pos)
    ka = rope(ka.reshape(b, l, RET_HEADS, RET_DK), pos) * (RET_DK ** -0.5)
    va = va.reshape(b, l, RET_HEADS, RET_DV)
    o_a, ret_new = retention(qa, ka, va, ret_s0)
    o_a = jax.nn.silu(ga.astype(jnp.float32)) * head_group_norm(o_a, ret_gn)
    qb = rope(rms_norm(qb.reshape(b, l, SWA_HEADS, SWA_DH), swa_qnorm), pos)
    kb = rope(rms_norm(kb.reshape(b, l, SWA_KV_HEADS, SWA_DH), swa_knorm), pos)
    vb = vb.reshape(b, l, SWA_KV_HEADS, SWA_DH)
    if swa_past is None:
        n = l // CHUNK
        pad = ((0, 0), (WINDOW_CHUNKS, 0), (0, 0), (0, 0), (0, 0))
        kp = jnp.pad(kb.reshape(b, n, CHUNK, SWA_KV_HEADS, SWA_DH), pad)
        vp = jnp.pad(vb.reshape(b, n, CHUNK, SWA_KV_HEADS, SWA_DH), pad)
        k_band = jnp.concatenate([kp[:, j:j + n] for j in range(WINDOW_CHUNKS + 1)], axis=2)
        v_band = jnp.concatenate([vp[:, j:j + n] for j in range(WINDOW_CHUNKS + 1)], axis=2)
        valid = (jnp.arange(n)[:, None] + jnp.arange(WINDOW_CHUNKS + 1)[None, :] - WINDOW_CHUNKS) >= 0
        mask = jnp.repeat(valid, CHUNK, axis=1)[None, :, None, None, None, :]
        o_b = sink_attention(qb.reshape(b, n, CHUNK, SWA_HEADS, SWA_DH), k_band, v_band, mask, swa_sinks)
        new_k, new_v = kb[:, l - WINDOW:], vb[:, l - WINDOW:]
    else:
        past_k, past_v = swa_past
        k_full = jnp.concatenate([past_k.astype(kb.dtype), kb], axis=1)
        v_full = jnp.concatenate([past_v.astype(vb.dtype), vb], axis=1)
        o_b = sink_attention(qb[:, None], k_full[:, None], v_full[:, None], None, swa_sinks)
        keep = past_k.shape[1]
        new_k, new_v = k_full[:, -keep:], v_full[:, -keep:]
    o_b = o_b.reshape(b, l, SWA_HEADS * SWA_DH)
    out = jnp.concatenate([o_a.astype(h.dtype), o_b.astype(h.dtype)], axis=-1) @ w_out0
    return out, ret_new, new_k, new_v


def s5_discretize(a_re, a_im, log_dt, b_re, b_im):
    dt = jnp.exp(log_dt.astype(jnp.float32))[:, None]
    are, aim = a_re.astype(jnp.float32), a_im.astype(jnp.float32)
    mag = jnp.exp(are * dt)
    lb_re, lb_im = mag * jnp.cos(aim * dt), mag * jnp.sin(aim * dt)
    den = are * are + aim * aim
    nr, ni = lb_re - 1.0, lb_im
    f_re = (nr * are + ni * aim) / den
    f_im = (ni * are - nr * aim) / den
    br, bi = b_re.astype(jnp.float32), b_im.astype(jnp.float32)
    bb_re = f_re[..., None] * br - f_im[..., None] * bi
    bb_im = f_re[..., None] * bi + f_im[..., None] * br
    return lb_re, lb_im, bb_re, bb_im


def s5_scan(u, s0_re, s0_im, lb_re, lb_im, bb_re, bb_im, c_re, c_im):
    bu_re = jnp.einsum('gpc,blgc->blgp', bb_re, u)
    bu_im = jnp.einsum('gpc,blgc->blgp', bb_im, u)
    a_re = jnp.broadcast_to(lb_re, bu_re.shape)
    a_im = jnp.broadcast_to(lb_im, bu_im.shape)

    def combine(e1, e2):
        a1r, a1i, b1r, b1i = e1
        a2r, a2i, b2r, b2i = e2
        return (a2r * a1r - a2i * a1i, a2r * a1i + a2i * a1r,
                a2r * b1r - a2i * b1i + b2r, a2r * b1i + a2i * b1r + b2i)

    pr, pi, hr, hi = lax.associative_scan(combine, (a_re, a_im, bu_re, bu_im), axis=1)
    s0r = s0_re.astype(jnp.float32)[:, None]
    s0i = s0_im.astype(jnp.float32)[:, None]
    hr = hr + pr * s0r - pi * s0i
    hi = hi + pr * s0i + pi * s0r
    y = (jnp.einsum('gcp,blgp->blgc', c_re.astype(jnp.float32), hr)
         - jnp.einsum('gcp,blgp->blgc', c_im.astype(jnp.float32), hi))
    return y, hr[:, -1], hi[:, -1]


def odd_mixer(h, ssm_s0_re, ssm_s0_im, conv_prev, w_in1, ssm_a_re, ssm_a_im, ssm_log_dt, ssm_b_re,
              ssm_b_im, ssm_c_re, ssm_c_im, ssm_d, ssm_w_glu, conv_w, conv_b, conv_ln_g, conv_ln_b, w_out1):
    b, l, _ = h.shape
    z = h @ w_in1
    u_c, d_val, d_gate = jnp.split(z, [SSM_WIDTH, SSM_WIDTH + CONV_WIDTH], axis=-1)
    lb_re, lb_im, bb_re, bb_im = s5_discretize(ssm_a_re, ssm_a_im, ssm_log_dt, ssm_b_re, ssm_b_im)
    uc = u_c.astype(jnp.float32).reshape(b, l, SSM_GROUPS, SSM_GROUP)
    y_c, s_re, s_im = s5_scan(uc, ssm_s0_re, ssm_s0_im, lb_re, lb_im, bb_re, bb_im, ssm_c_re, ssm_c_im)
    y_c = y_c.reshape(b, l, SSM_WIDTH) + ssm_d.astype(jnp.float32) * u_c.astype(jnp.float32)
    g = jax.nn.gelu(y_c, approximate=False)
    o_c = g * jax.nn.sigmoid(g @ ssm_w_glu.astype(jnp.float32))
    u_d = d_val * jax.nn.sigmoid(d_gate)
    xp = jnp.concatenate([conv_prev.astype(u_d.dtype), u_d], axis=1)
    y_d = lax.conv_general_dilated(xp, conv_w.astype(xp.dtype)[:, None, :], window_strides=(1,),
                                   padding='VALID', dimension_numbers=('NWC', 'WIO', 'NWC'),
                                   feature_group_count=CONV_WIDTH) + conv_b.astype(xp.dtype)
    o_d = jax.nn.silu(layer_norm(y_d, conv_ln_g, conv_ln_b))
    new_conv = xp[:, -(CONV_K - 1):]
    out = jnp.concatenate([o_c.astype(h.dtype), o_d.astype(h.dtype)], axis=-1) @ w_out1
    return out, s_re, s_im, new_conv


def setup_inputs(seed: int = 0) -> dict:
    key = jax.random.key(seed)
    ks = iter(jax.random.split(key, 64))

    def nrm(shape, scale):
        return scale * jax.random.normal(next(ks), shape, jnp.float32)

    win = min(WINDOW, PAST_LEN)
    d = D_MODEL
    return {
        'x_prompt': nrm((BATCH, SEQ, d), 1.0),
        'x_sample': nrm((DEC_BATCH, DEC_SEQ, d), 1.0),
        'cache_k_swa': nrm((DEC_BATCH, win, SWA_KV_HEADS, SWA_DH), 1.0),
        'cache_v_swa': nrm((DEC_BATCH, win, SWA_KV_HEADS, SWA_DH), 1.0),
        'state_ret': nrm((DEC_BATCH, RET_HEADS, RET_DK, RET_DV), 0.1),
        'state_ssm_re': nrm((DEC_BATCH, SSM_GROUPS, SSM_STATE), 0.3),
        'state_ssm_im': nrm((DEC_BATCH, SSM_GROUPS, SSM_STATE), 0.3),
        'state_conv': nrm((DEC_BATCH, CONV_K - 1, CONV_WIDTH), 0.5),
        'norm0_mix': 1.0 + nrm((d,), 0.01),
        'w_in0': nrm((d, IN0_COLS), d ** -0.5),
        'ret_gn': 1.0 + nrm((RET_HEADS * RET_DV,), 0.01),
        'swa_qnorm': 1.0 + nrm((SWA_DH,), 0.01),
        'swa_knorm': 1.0 + nrm((SWA_DH,), 0.01),
        'swa_sinks': nrm((SWA_HEADS,), 0.5),
        'w_out0': nrm((OUT0_COLS, d), OUT0_COLS ** -0.5),
        'norm0_ffn': 1.0 + nrm((d,), 0.01),
        'ffn0_w_gate': nrm((d, D_FF), d ** -0.5),
        'ffn0_w_up': nrm((d, D_FF), d ** -0.5),
        'ffn0_w_down': nrm((D_FF, d), D_FF ** -0.5),
        'norm1_mix': 1.0 + nrm((d,), 0.01),
        'w_in1': nrm((d, IN1_COLS), d ** -0.5),
        'ssm_a_re': -0.5 + nrm((SSM_GROUPS, SSM_STATE), 0.01),
        'ssm_a_im': math.pi * jnp.arange(SSM_STATE, dtype=jnp.float32)[None, :] + nrm((SSM_GROUPS, SSM_STATE), 0.01),
        'ssm_log_dt': jax.random.uniform(next(ks), (SSM_GROUPS,), jnp.float32, math.log(1e-3), math.log(1e-1)),
        'ssm_b_re': nrm((SSM_GROUPS, SSM_STATE, SSM_GROUP), (2.0 * SSM_GROUP) ** -0.5),
        'ssm_b_im': nrm((SSM_GROUPS, SSM_STATE, SSM_GROUP), (2.0 * SSM_GROUP) ** -0.5),
        'ssm_c_re': nrm((SSM_GROUPS, SSM_GROUP, SSM_STATE), (2.0 * SSM_STATE) ** -0.5),
        'ssm_c_im': nrm((SSM_GROUPS, SSM_GROUP, SSM_STATE), (2.0 * SSM_STATE) ** -0.5),
        'ssm_d': nrm((SSM_WIDTH,), 1.0),
        'ssm_w_glu': nrm((SSM_WIDTH, SSM_WIDTH), SSM_WIDTH ** -0.5),
        'conv_w': nrm((CONV_K, CONV_WIDTH), CONV_K ** -0.5),
        'conv_b': nrm((CONV_WIDTH,), 0.02),
        'conv_ln_g': 1.0 + nrm((CONV_WIDTH,), 0.01),
        'conv_ln_b': nrm((CONV_WIDTH,), 0.02),
        'w_out1': nrm((OUT1_COLS, d), OUT1_COLS ** -0.5),
        'norm1_ffn': 1.0 + nrm((d,), 0.01),
        'moe_router': nrm((d, N_EXPERTS), d ** -0.5),
        'moe_w_gate': nrm((N_EXPERTS, d, D_FF_EXPERT), d ** -0.5),
        'moe_w_up': nrm((N_EXPERTS, d, D_FF_EXPERT), d ** -0.5),
        'moe_w_down': nrm((N_EXPERTS, D_FF_EXPERT, d), D_FF_EXPERT ** -0.5),
    }


def reference(x_prompt, x_sample, cache_k_swa, cache_v_swa, state_ret, state_ssm_re, state_ssm_im, state_conv,
              norm0_mix, w_in0, ret_gn, swa_qnorm, swa_knorm, swa_sinks, w_out0, norm0_ffn, ffn0_w_gate,
              ffn0_w_up, ffn0_w_down, norm1_mix, w_in1, ssm_a_re, ssm_a_im, ssm_log_dt, ssm_b_re, ssm_b_im,
              ssm_c_re, ssm_c_im, ssm_d, ssm_w_glu, conv_w, conv_b, conv_ln_g, conv_ln_b, w_out1, norm1_ffn,
              moe_router, moe_w_gate, moe_w_up, moe_w_down):
    bp, lp, _ = x_prompt.shape
    ls = x_sample.shape[1]
    pos_p = jnp.arange(lp, dtype=jnp.int32)
    pos_s = PAST_LEN + jnp.arange(ls, dtype=jnp.int32)
    ret_zero = jnp.zeros((bp, RET_HEADS, RET_DK, RET_DV), jnp.float32)
    ssm_zero = jnp.zeros((bp, SSM_GROUPS, SSM_STATE), jnp.float32)
    conv_zero = jnp.zeros((bp, CONV_K - 1, CONV_WIDTH), x_prompt.dtype)
    h_p, h_s = x_prompt, x_sample
    for layer in range(DEPTH):
        if layer % 2 == 0:
            mix_p, ret_p, k_swa_p, v_swa_p = even_mixer(
                rms_norm(h_p, norm0_mix), pos_p, ret_zero, None,
                w_in0, ret_gn, swa_qnorm, swa_knorm, swa_sinks, w_out0)
            mix_s, ret_s, k_swa_s, v_swa_s = even_mixer(
                rms_norm(h_s, norm0_mix), pos_s, state_ret, (cache_k_swa, cache_v_swa),
                w_in0, ret_gn, swa_qnorm, swa_knorm, swa_sinks, w_out0)
            h_p = h_p + mix_p.astype(h_p.dtype)
            h_s = h_s + mix_s.astype(h_s.dtype)
            h_p = h_p + swiglu(rms_norm(h_p, norm0_ffn), ffn0_w_gate, ffn0_w_up, ffn0_w_down).astype(h_p.dtype)
            h_s = h_s + swiglu(rms_norm(h_s, norm0_ffn), ffn0_w_gate, ffn0_w_up, ffn0_w_down).astype(h_s.dtype)
        else:
            mix_p, ssm_re_p, ssm_im_p, conv_p = odd_mixer(
                rms_norm(h_p, norm1_mix), ssm_zero, ssm_zero, conv_zero, w_in1, ssm_a_re, ssm_a_im,
                ssm_log_dt, ssm_b_re, ssm_b_im, ssm_c_re, ssm_c_im, ssm_d, ssm_w_glu, conv_w, conv_b,
                conv_ln_g, conv_ln_b, w_out1)
            mix_s, ssm_re_s, ssm_im_s, conv_s = odd_mixer(
                rms_norm(h_s, norm1_mix), state_ssm_re, state_ssm_im, state_conv, w_in1, ssm_a_re, ssm_a_im,
                ssm_log_dt, ssm_b_re, ssm_b_im, ssm_c_re, ssm_c_im, ssm_d, ssm_w_glu, conv_w, conv_b,
                conv_ln_g, conv_ln_b, w_out1)
            h_p = h_p + mix_p.astype(h_p.dtype)
            h_s = h_s + mix_s.astype(h_s.dtype)
            h_p = h_p + moe_swiglu(rms_norm(h_p, norm1_ffn), moe_router, moe_w_gate, moe_w_up, moe_w_down).astype(h_p.dtype)
            h_s = h_s + moe_swiglu(rms_norm(h_s, norm1_ffn), moe_router, moe_w_gate, moe_w_up, moe_w_down).astype(h_s.dtype)
    return (h_p, h_s, k_swa_p, v_swa_p, k_swa_s, v_swa_s, ret_p, ret_s,
            ssm_re_p, ssm_im_p, ssm_re_s, ssm_im_s, conv_p, conv_s)
```

```python
import functools
import math

import jax
import jax.numpy as jnp
import numpy as np
from jax import lax
from jax.experimental import pallas as pl
from jax.experimental.pallas import tpu as pltpu

F32 = jnp.float32
BF16 = jnp.bfloat16

D_MODEL = 1024
CHUNK = 64
EPS = 1e-6
ROPE_THETA = 10000.0
PAST_LEN = 4096
RET_HEADS, RET_DK, RET_DV = 4, 64, 128
SWA_HEADS, SWA_KV_HEADS, SWA_DH = 8, 2, 64
SWA_GROUP = SWA_HEADS // SWA_KV_HEADS
WINDOW = 128
SSM_WIDTH = D_MODEL // 2
SSM_GROUP = 16
SSM_GROUPS = SSM_WIDTH // SSM_GROUP
SSM_STATE = 64
SSM_COLS = SSM_GROUPS * SSM_STATE
CONV_WIDTH = D_MODEL // 2
CONV_K = 31
CONV_PAD = 32
N_EXPERTS = 8
TOP_K = 2

QA0, KA0, VA0, GA0 = 0, 256, 512, 1024
QB0, KB0, VB0 = 1536, 2048, 2176
IN0_COLS = 2304
OUT0_COLS = 1024
IN1_COLS = 1536

LANES = 128
VMEM_LIMIT = 56 * 1024 * 1024


def _rms_norm(x, g):
    return x * lax.rsqrt(jnp.mean(x * x, axis=-1, keepdims=True) + EPS) * g


def _silu(x):
    return x * jax.nn.sigmoid(x)


def _dot(a, b):
    return jnp.dot(a, b, preferred_element_type=F32)


def _dot_nt(a, b):
    return lax.dot_general(a, b, (((1,), (1,)), ((), ())), preferred_element_type=F32)


def _dot_tn(a, b):
    return lax.dot_general(a, b, (((0,), (0,)), ((), ())), preferred_element_type=F32)


def _rot_half(x):
    w = x.shape[-1]
    lane = lax.broadcasted_iota(jnp.int32, x.shape, x.ndim - 1)
    fwd = pltpu.roll(x, w - SWA_DH // 2, x.ndim - 1)
    bwd = pltpu.roll(x, SWA_DH // 2, x.ndim - 1)
    return jnp.where(lane % SWA_DH < SWA_DH // 2, fwd, bwd)


def _rope(x, cos, sin_signed):
    reps = x.shape[-1] // LANES
    if reps > 1:
        cos = jnp.concatenate([cos] * reps, axis=-1)
        sin_signed = jnp.concatenate([sin_signed] * reps, axis=-1)
    return x * cos + _rot_half(x) * sin_signed


def _split_dot(x, w_bf16):
    hi = x.astype(BF16)
    lo = (x - hi.astype(F32)).astype(BF16)
    return _dot(hi, w_bf16) + _dot(lo, w_bf16)


def _even_kernel(sink_ref, x_ref, g_ref, win_ref, wout_ref, cos_ref, sin_ref, intra_ref, qdec_ref, kdec_ref,
                 gn_ref, qn_ref, kn_ref, ones_ref, kprev_ref, vprev_ref, s0_ref,
                 o_ref, knew_ref, vnew_ref, sfin_ref,
                 z_ref, mix_ref, kwin_ref, vwin_ref, s_ref, *, chunks, mask_initial, chunk_decay):
    t = pl.program_id(1)

    @pl.when(t == 0)
    def _():
        kwin_ref[0:WINDOW, :] = kprev_ref[...]
        vwin_ref[0:WINDOW, :] = vprev_ref[...]
        s_ref[...] = s0_ref[...]

    x = x_ref[...]
    xn = _rms_norm(x, g_ref[...]).astype(BF16)
    z_ref[...] = _dot(xn, win_ref[...])

    cos = cos_ref[...]
    sin = sin_ref[...]
    z_ref[:, QA0:KA0] = _rope(z_ref[:, QA0:KA0], cos, sin)
    z_ref[:, KA0:VA0] = _rope(z_ref[:, KA0:VA0], cos, sin) * (RET_DK ** -0.5)
    qk = z_ref[:, QB0:VB0]
    ms = _split_dot(qk * qk, ones_ref[...]) * (1.0 / SWA_DH)
    qk = qk * lax.rsqrt(ms + EPS)
    nq = SWA_HEADS * SWA_DH
    z_ref[:, QB0:KB0] = _rope(qk[:, :nq] * qn_ref[...], cos, sin)
    z_ref[:, KB0:VB0] = _rope(qk[:, nq:] * kn_ref[...], cos, sin)

    row = lax.broadcasted_iota(jnp.int32, (SWA_GROUP * CHUNK, 1), 0)
    key_pos = lax.broadcasted_iota(jnp.int32, (1, WINDOW + CHUNK), 1)

    def chunk_step(c, carry):
        r0 = pl.multiple_of(c * CHUNK, CHUNK)
        rows = pl.ds(r0, CHUNK)
        for h in range(RET_HEADS):
            q = z_ref[rows, QA0 + h * RET_DK:QA0 + (h + 1) * RET_DK]
            k = z_ref[rows, KA0 + h * RET_DK:KA0 + (h + 1) * RET_DK]
            v = z_ref[rows, VA0 + h * RET_DV:VA0 + (h + 1) * RET_DV].astype(BF16)
            gate = z_ref[rows, GA0 + h * RET_DV:GA0 + (h + 1) * RET_DV]
            qd = q * qdec_ref[:, h * RET_DK:(h + 1) * RET_DK]
            kd = k * kdec_ref[:, h * RET_DK:(h + 1) * RET_DK]
            scores = _dot_nt(q.astype(BF16), k.astype(BF16)) * intra_ref[h]
            state = s_ref[h]
            o = _dot(scores.astype(BF16), v) + _dot(qd.astype(BF16), state.astype(BF16))
            s_ref[h] = chunk_decay[h] * state + _dot_tn(kd.astype(BF16), v)
            oc = o - jnp.mean(o, axis=-1, keepdims=True)
            on = oc * lax.rsqrt(jnp.mean(oc * oc, axis=-1, keepdims=True) + EPS)
            on = on * gn_ref[:, h * RET_DV:(h + 1) * RET_DV]
            mix_ref[rows, h * RET_DV:(h + 1) * RET_DV] = (_silu(gate) * on).astype(BF16)
        kwin_ref[WINDOW:, :] = z_ref[rows, KB0:VB0]
        vwin_ref[WINDOW:, :] = z_ref[rows, VB0:IN0_COLS]
        if mask_initial:
            first_valid = WINDOW - (t * chunks + c) * CHUNK
        for j in range(SWA_KV_HEADS):
            keys = kwin_ref[:, j * SWA_DH:(j + 1) * SWA_DH].astype(BF16)
            vals = vwin_ref[:, j * SWA_DH:(j + 1) * SWA_DH].astype(BF16)
            heads = [SWA_GROUP * j + g for g in range(SWA_GROUP)]
            qg = jnp.concatenate(
                [z_ref[rows, QB0 + hh * SWA_DH:QB0 + (hh + 1) * SWA_DH] for hh in heads], axis=0)
            s = _dot_nt(qg.astype(BF16), keys) * (SWA_DH ** -0.5)
            if mask_initial:
                s = jnp.where(key_pos >= first_valid, s, -1e30)
            sink = jnp.zeros((SWA_GROUP * CHUNK, 1), F32)
            for g, hh in enumerate(heads):
                sink = jnp.where(row // CHUNK == g, sink_ref[hh], sink)
            m = jnp.maximum(jnp.max(s, axis=-1, keepdims=True), sink)
            p = jnp.exp(s - m)
            den = jnp.sum(p, axis=-1, keepdims=True) + jnp.exp(sink - m)
            o = _dot(p.astype(BF16), vals) / den
            for g, hh in enumerate(heads):
                c0 = RET_HEADS * RET_DV + hh * SWA_DH
                mix_ref[rows, c0:c0 + SWA_DH] = o[g * CHUNK:(g + 1) * CHUNK, :].astype(BF16)
        kwin_ref[0:WINDOW, :] = kwin_ref[CHUNK:, :]
        vwin_ref[0:WINDOW, :] = vwin_ref[CHUNK:, :]
        return carry

    lax.fori_loop(0, chunks, chunk_step, 0)

    o_ref[...] = x + _dot(mix_ref[...], wout_ref[...])
    knew_ref[...] = kwin_ref[0:WINDOW, :]
    vnew_ref[...] = vwin_ref[0:WINDOW, :]
    sfin_ref[...] = s_ref[...]


def _retention_tables():
    h = np.arange(RET_HEADS, dtype=np.float64)
    log_g = np.log1p(-np.exp2(-5.0 - h))
    idx = np.arange(CHUNK, dtype=np.float64)
    intra = np.exp(np.abs(idx[:, None] - idx[None, :])[None] * log_g[:, None, None])
    q_dec = np.exp((idx + 1.0)[:, None] * log_g[None, :])
    k_dec = np.exp((CHUNK - 1.0 - idx)[:, None] * log_g[None, :])
    chunk_dec = tuple(float(v) for v in np.exp(CHUNK * log_g))
    q_dec = np.repeat(q_dec, RET_DK, axis=1)
    k_dec = np.repeat(k_dec, RET_DK, axis=1)
    return (jnp.asarray(intra, F32), jnp.asarray(q_dec, F32), jnp.asarray(k_dec, F32), chunk_dec)


def _rope_tables(pos):
    half = SWA_DH // 2
    inv = jnp.exp(-math.log(ROPE_THETA) * jnp.arange(half, dtype=F32) / half)
    ang = pos.astype(F32)[:, None] * inv[None, :]
    cos, sin = jnp.cos(ang), jnp.sin(ang)
    cos = jnp.concatenate([cos, cos] * (LANES // SWA_DH), axis=-1)
    sin = jnp.concatenate([-sin, sin] * (LANES // SWA_DH), axis=-1)
    return cos, sin


def _even_mixer(x, pos, kprev, vprev, s0, mask_initial, p):
    b, l, d = x.shape
    tq = min(512, l)
    chunks = tq // CHUNK
    intra, q_dec, k_dec, chunk_decay = _retention_tables()
    cos, sin = _rope_tables(pos)
    nqk = (SWA_HEADS + SWA_KV_HEADS) * SWA_DH
    group = np.arange(nqk) // SWA_DH
    ones_blk = jnp.asarray(group[:, None] == group[None, :], BF16)

    def const(shape):
        return pl.BlockSpec(shape, lambda bi, ti: (0,) * len(shape))

    kv_spec = pl.BlockSpec((None, WINDOW, LANES), lambda bi, ti: (bi, 0, 0))
    st_spec = pl.BlockSpec((None, RET_HEADS, RET_DK, RET_DV), lambda bi, ti: (bi, 0, 0, 0))
    x_spec = pl.BlockSpec((None, tq, d), lambda bi, ti: (bi, ti, 0))
    tab_spec = pl.BlockSpec((tq, LANES), lambda bi, ti: (ti, 0))
    kern = functools.partial(_even_kernel, chunks=chunks, mask_initial=mask_initial, chunk_decay=chunk_decay)
    return pl.pallas_call(
        kern,
        grid=(b, l // tq),
        in_specs=[
            pl.BlockSpec(memory_space=pltpu.SMEM),
            x_spec, const((1, d)), const((d, IN0_COLS)), const((OUT0_COLS, d)),
            tab_spec, tab_spec,
            const((RET_HEADS, CHUNK, CHUNK)), const((CHUNK, RET_HEADS * RET_DK)), const((CHUNK, RET_HEADS * RET_DK)),
            const((1, RET_HEADS * RET_DV)), const((1, SWA_HEADS * SWA_DH)), const((1, SWA_KV_HEADS * SWA_DH)),
            const((nqk, nqk)), kv_spec, kv_spec, st_spec,
        ],
        out_specs=[x_spec, kv_spec, kv_spec, st_spec],
        out_shape=[
            jax.ShapeDtypeStruct((b, l, d), F32),
            jax.ShapeDtypeStruct((b, WINDOW, LANES), F32),
            jax.ShapeDtypeStruct((b, WINDOW, LANES), F32),
            jax.ShapeDtypeStruct((b, RET_HEADS, RET_DK, RET_DV), F32),
        ],
        scratch_shapes=[
            pltpu.VMEM((tq, IN0_COLS), F32),
            pltpu.VMEM((tq, OUT0_COLS), BF16),
            pltpu.VMEM((WINDOW + CHUNK, LANES), F32),
            pltpu.VMEM((WINDOW + CHUNK, LANES), F32),
            pltpu.VMEM((RET_HEADS, RET_DK, RET_DV), F32),
        ],
        compiler_params=pltpu.CompilerParams(
            dimension_semantics=("arbitrary", "arbitrary"), vmem_limit_bytes=VMEM_LIMIT),
        name="even_mixer",
    )(p["sinks"], x, p["norm0_mix"], p["w_in0"], p["w_out0"], cos, sin, intra, q_dec, k_dec,
      p["ret_gn"], p["qnorm"], p["knorm"], ones_blk, kprev, vprev, s0)


SSM_SLABS = SSM_COLS // LANES
SCAN_SLABS = 4
BATCH_ROWS = 8


def _odd_kernel(x_ref, g_ref, win_ref, bbd_ref, lre_ref, lim_ref, cbd_ref, d_ref, wglu_ref, cw_ref, cb_ref,
                lng_ref, lnb_ref, wout_ref, s0re_ref, s0im_ref, conv0_ref,
                o_ref, sre_ref, sim_ref, convn_ref,
                bu_ref, hre_ref, him_ref, xp_ref, *, tm):
    t = pl.program_id(1)
    rows = BATCH_ROWS * tm

    @pl.when(t == 0)
    def _():
        for j in range(SSM_SLABS):
            hre_ref[j] = s0re_ref[:, j * LANES:(j + 1) * LANES]
            him_ref[j] = s0im_ref[:, j * LANES:(j + 1) * LANES]
        xp_ref[:, 0:CONV_PAD, :] = conv0_ref[...]

    x = x_ref[...].reshape(rows, D_MODEL)
    xn = _rms_norm(x, g_ref[...]).astype(BF16)
    z = _dot(xn, win_ref[...])
    u = z[:, 0:SSM_WIDTH]
    d_val = z[:, SSM_WIDTH:SSM_WIDTH + CONV_WIDTH]
    d_gate = z[:, SSM_WIDTH + CONV_WIDTH:]

    bu = _dot(u.astype(BF16), bbd_ref[...])
    for j in range(2 * SSM_SLABS):
        bu_ref[j] = bu[:, j * LANES:(j + 1) * LANES]
    for j0 in range(0, SSM_SLABS, SCAN_SLABS):
        slabs = range(j0, j0 + SCAN_SLABS)
        lr = [lre_ref[j] for j in slabs]
        li = [lim_ref[j] for j in slabs]

        def step(ti, carry):
            at_t = pl.ds(ti, BATCH_ROWS, stride=tm)
            out = []
            for n, j in enumerate(slabs):
                hr, hi = carry[n]
                nr = lr[n] * hr - li[n] * hi + bu_ref[j, at_t, :]
                ni = lr[n] * hi + li[n] * hr + bu_ref[SSM_SLABS + j, at_t, :]
                bu_ref[j, at_t, :] = nr
                bu_ref[SSM_SLABS + j, at_t, :] = ni
                out.append((nr, ni))
            return tuple(out)

        init = tuple((hre_ref[j], him_ref[j]) for j in slabs)
        final = lax.fori_loop(0, tm, step, init, unroll=4)
        for n, j in enumerate(slabs):
            hre_ref[j], him_ref[j] = final[n]

    states = jnp.concatenate([bu_ref[j].astype(BF16) for j in range(2 * SSM_SLABS)], axis=-1)
    y = _dot(states, cbd_ref[...]) + d_ref[...] * u
    g = 0.5 * y * (1.0 + lax.erf(y * (2.0 ** -0.5)))
    o_c = g * jax.nn.sigmoid(_dot(g.astype(BF16), wglu_ref[...]))

    u_d = d_val * jax.nn.sigmoid(d_gate)
    xp_ref[:, CONV_PAD:, :] = u_d.reshape(BATCH_ROWS, tm, CONV_WIDTH)
    first = CONV_PAD - (CONV_K - 1)
    acc = jnp.zeros((BATCH_ROWS, tm, CONV_WIDTH), F32) + cb_ref[...]
    for k in range(CONV_K):
        acc = acc + xp_ref[:, first + k:first + k + tm, :] * cw_ref[k:k + 1, :]
    mu = jnp.mean(acc, axis=-1, keepdims=True)
    xc = acc - mu
    var = jnp.mean(xc * xc, axis=-1, keepdims=True)
    o_d = _silu(xc * lax.rsqrt(var + EPS) * lng_ref[...] + lnb_ref[...]).reshape(rows, CONV_WIDTH)
    xp_ref[:, 0:CONV_PAD, :] = xp_ref[:, tm:tm + CONV_PAD, :]

    mix = jnp.concatenate([o_c.astype(BF16), o_d.astype(BF16)], axis=-1)
    o_ref[...] = (x + _dot(mix, wout_ref[...])).reshape(BATCH_ROWS, tm, D_MODEL)
    for j in range(SSM_SLABS):
        sre_ref[:, j * LANES:(j + 1) * LANES] = hre_ref[j]
        sim_ref[:, j * LANES:(j + 1) * LANES] = him_ref[j]
    convn_ref[...] = xp_ref[:, 0:CONV_PAD, :]


def _s5_discretize(p):
    dt = jnp.exp(p["ssm_log_dt"].astype(F32))[:, None]
    are, aim = p["ssm_a_re"].astype(F32), p["ssm_a_im"].astype(F32)
    mag = jnp.exp(are * dt)
    lb_re, lb_im = mag * jnp.cos(aim * dt), mag * jnp.sin(aim * dt)
    den = are * are + aim * aim
    nr, ni = lb_re - 1.0, lb_im
    f_re = (nr * are + ni * aim) / den
    f_im = (ni * are - nr * aim) / den
    br, bi = p["ssm_b_re"].astype(F32), p["ssm_b_im"].astype(F32)
    bb_re = f_re[..., None] * br - f_im[..., None] * bi
    bb_im = f_re[..., None] * bi + f_im[..., None] * br
    return lb_re, lb_im, bb_re, bb_im


def _s5_operands(p):
    lb_re, lb_im, bb_re, bb_im = _s5_discretize(p)
    eye = jnp.eye(SSM_GROUPS, dtype=F32)

    def in_map(bb):
        return jnp.einsum("gpc,gh->gchp", bb, eye).reshape(SSM_WIDTH, SSM_COLS)

    def out_map(c):
        return jnp.einsum("gcp,gh->gphc", c.astype(F32), eye).reshape(SSM_COLS, SSM_WIDTH)

    bbd = jnp.concatenate([in_map(bb_re), in_map(bb_im)], axis=1).astype(BF16)
    cbd = jnp.concatenate([out_map(p["ssm_c_re"]), -out_map(p["ssm_c_im"])], axis=0).astype(BF16)
    slab_shape = (SSM_SLABS, BATCH_ROWS, LANES)
    lre = jnp.broadcast_to(lb_re.reshape(SSM_SLABS, 1, LANES), slab_shape)
    lim = jnp.broadcast_to(lb_im.reshape(SSM_SLABS, 1, LANES), slab_shape)
    return bbd, cbd, lre, lim


def _odd_mixer(x, s0re, s0im, conv0, p):
    b, l, d = x.shape
    tm = min(64, l)
    bbd, cbd, lre, lim = _s5_operands(p)
    conv0 = jnp.pad(conv0, ((0, 0), (CONV_PAD - (CONV_K - 1), 0), (0, 0)))
    conv_w = jnp.pad(p["conv_w"], ((0, CONV_PAD - CONV_K), (0, 0)))

    def const(shape):
        return pl.BlockSpec(shape, lambda bi, ti: (0,) * len(shape))

    x_spec = pl.BlockSpec((BATCH_ROWS, tm, d), lambda bi, ti: (bi, ti, 0))
    st_spec = pl.BlockSpec((BATCH_ROWS, SSM_COLS), lambda bi, ti: (bi, 0))
    cv_spec = pl.BlockSpec((BATCH_ROWS, CONV_PAD, CONV_WIDTH), lambda bi, ti: (bi, 0, 0))
    out, sre, sim, convn = pl.pallas_call(
        functools.partial(_odd_kernel, tm=tm),
        grid=(b // BATCH_ROWS, l // tm),
        in_specs=[
            x_spec, const((1, d)), const((d, IN1_COLS)), const((SSM_WIDTH, 2 * SSM_COLS)),
            const((SSM_SLABS, BATCH_ROWS, LANES)), const((SSM_SLABS, BATCH_ROWS, LANES)),
            const((2 * SSM_COLS, SSM_WIDTH)),
            const((1, SSM_WIDTH)), const((SSM_WIDTH, SSM_WIDTH)), const((CONV_PAD, CONV_WIDTH)),
            const((1, CONV_WIDTH)), const((1, CONV_WIDTH)), const((1, CONV_WIDTH)),
            const((SSM_WIDTH + CONV_WIDTH, d)), st_spec, st_spec, cv_spec,
        ],
        out_specs=[x_spec, st_spec, st_spec, cv_spec],
        out_shape=[
            jax.ShapeDtypeStruct((b, l, d), F32),
            jax.ShapeDtypeStruct((b, SSM_COLS), F32),
            jax.ShapeDtypeStruct((b, SSM_COLS), F32),
            jax.ShapeDtypeStruct((b, CONV_PAD, CONV_WIDTH), F32),
        ],
        scratch_shapes=[
            pltpu.VMEM((2 * SSM_SLABS, BATCH_ROWS * tm, LANES), F32),
            pltpu.VMEM((SSM_SLABS, BATCH_ROWS, LANES), F32),
            pltpu.VMEM((SSM_SLABS, BATCH_ROWS, LANES), F32),
            pltpu.VMEM((BATCH_ROWS, CONV_PAD + tm, CONV_WIDTH), F32),
        ],
        compiler_params=pltpu.CompilerParams(
            dimension_semantics=("arbitrary", "arbitrary"), vmem_limit_bytes=VMEM_LIMIT),
        name="odd_mixer",
    )(x, p["norm1_mix"], p["w_in1"], bbd, lre, lim, cbd, p["ssm_d"], p["ssm_w_glu"], conv_w, p["conv_b"],
      p["conv_ln_g"], p["conv_ln_b"], p["w_out1"], s0re, s0im, conv0)
    shape3 = (b, SSM_GROUPS, SSM_STATE)
    return out, sre.reshape(shape3), sim.reshape(shape3), convn[:, CONV_PAD - (CONV_K - 1):, :]


def _ffn_kernel(x_ref, g_ref, wg_ref, wu_ref, wd_ref, o_ref, xn_ref):
    j = pl.program_id(1)

    @pl.when(j == 0)
    def _():
        x = x_ref[...]
        xn_ref[...] = _rms_norm(x, g_ref[...]).astype(BF16)
        o_ref[...] = x

    xn = xn_ref[...]
    hidden = _silu(_dot(xn, wg_ref[...])) * _dot(xn, wu_ref[...])
    o_ref[...] += _dot(hidden.astype(BF16), wd_ref[...])


def _dense_ffn(x, g, wg, wu, wd, *, tm, tf):
    n, d = x.shape
    f = wg.shape[1]
    x_spec = pl.BlockSpec((tm, d), lambda i, j: (i, 0))
    return pl.pallas_call(
        _ffn_kernel,
        grid=(n // tm, f // tf),
        in_specs=[
            x_spec, pl.BlockSpec((1, d), lambda i, j: (0, 0)),
            pl.BlockSpec((d, tf), lambda i, j: (0, j)), pl.BlockSpec((d, tf), lambda i, j: (0, j)),
            pl.BlockSpec((tf, d), lambda i, j: (j, 0)),
        ],
        out_specs=x_spec,
        out_shape=jax.ShapeDtypeStruct((n, d), F32),
        scratch_shapes=[pltpu.VMEM((tm, d), BF16)],
        compiler_params=pltpu.CompilerParams(
            dimension_semantics=("arbitrary", "arbitrary"), vmem_limit_bytes=VMEM_LIMIT),
        name="dense_ffn",
    )(x, g, wg, wu, wd)


def _moe_kernel(x_ref, g_ref, r_hi_ref, r_lo_ref, wg_ref, wu_ref, wd_ref, o_ref, xn_ref, gate_ref, acc_ref,
                *, tiles_per_expert):
    j = pl.program_id(1)
    expert = j // tiles_per_expert
    tile = j % tiles_per_expert

    @pl.when(j == 0)
    def _():
        x = x_ref[...]
        xn = _rms_norm(x, g_ref[...])
        xn_ref[...] = xn.astype(BF16)
        o_ref[...] = x
        logits = _split_dot(xn, r_hi_ref[...]) + _dot(xn.astype(BF16), r_lo_ref[...])
        lane = lax.broadcasted_iota(jnp.int32, logits.shape, 1)
        logits = jnp.where(lane < N_EXPERTS, logits, -jnp.inf)
        v1 = jnp.max(logits, axis=-1, keepdims=True)
        i1 = jnp.min(jnp.where(logits == v1, lane, LANES), axis=-1, keepdims=True)
        rest = jnp.where(lane == i1, -jnp.inf, logits)
        v2 = jnp.max(rest, axis=-1, keepdims=True)
        i2 = jnp.min(jnp.where(rest == v2, lane, LANES), axis=-1, keepdims=True)
        e2 = jnp.exp(v2 - v1)
        den = 1.0 + e2
        gate_ref[...] = jnp.where(lane == i1, 1.0 / den, jnp.where(lane == i2, e2 / den, 0.0))

    @pl.when(tile == 0)
    def _():
        acc_ref[...] = jnp.zeros_like(acc_ref)

    xn = xn_ref[...]
    hidden = _silu(_dot(xn, wg_ref[...])) * _dot(xn, wu_ref[...])
    acc_ref[...] += _dot(hidden.astype(BF16), wd_ref[...])

    @pl.when(tile == tiles_per_expert - 1)
    def _():
        gates = gate_ref[...]
        lane = lax.broadcasted_iota(jnp.int32, gates.shape, 1)
        gate = jnp.sum(jnp.where(lane == expert, gates, 0.0), axis=-1, keepdims=True)
        o_ref[...] += gate * acc_ref[...]


def _moe_ffn(x, g, r_hi, r_lo, wg, wu, wd, *, tm, tf):
    n, d = x.shape
    e, _, f = wg.shape
    tpe = f // tf
    x_spec = pl.BlockSpec((tm, d), lambda i, j: (i, 0))
    r_spec = pl.BlockSpec((d, LANES), lambda i, j: (0, 0))
    up_spec = pl.BlockSpec((None, d, tf), lambda i, j: (j // tpe, 0, j % tpe))
    return pl.pallas_call(
        functools.partial(_moe_kernel, tiles_per_expert=tpe),
        grid=(n // tm, e * tpe),
        in_specs=[
            x_spec, pl.BlockSpec((1, d), lambda i, j: (0, 0)), r_spec, r_spec, up_spec, up_spec,
            pl.BlockSpec((None, tf, d), lambda i, j: (j // tpe, j % tpe, 0)),
        ],
        out_specs=x_spec,
        out_shape=jax.ShapeDtypeStruct((n, d), F32),
        scratch_shapes=[pltpu.VMEM((tm, d), BF16), pltpu.VMEM((tm, LANES), F32), pltpu.VMEM((tm, d), F32)],
        compiler_params=pltpu.CompilerParams(
            dimension_semantics=("arbitrary", "arbitrary"), vmem_limit_bytes=VMEM_LIMIT),
        name="moe_ffn",
    )(x, g, r_hi, r_lo, wg, wu, wd)


def kernel(x_prompt, x_sample, cache_k_swa, cache_v_swa, state_ret, state_ssm_re, state_ssm_im, state_conv,
           norm0_mix, w_in0, ret_gn, swa_qnorm, swa_knorm, swa_sinks, w_out0, norm0_ffn, ffn0_w_gate,
           ffn0_w_up, ffn0_w_down, norm1_mix, w_in1, ssm_a_re, ssm_a_im, ssm_log_dt, ssm_b_re, ssm_b_im,
           ssm_c_re, ssm_c_im, ssm_d, ssm_w_glu, conv_w, conv_b, conv_ln_g, conv_ln_b, w_out1, norm1_ffn,
           moe_router, moe_w_gate, moe_w_up, moe_w_down):
    bp, lp, d = x_prompt.shape
    bs, ls, _ = x_sample.shape
    row = lambda v: v.astype(F32).reshape(1, -1)
    r_pad = jnp.pad(moe_router.astype(F32), ((0, 0), (0, LANES - N_EXPERTS)))
    r_hi = r_pad.astype(BF16)
    p = dict(
        sinks=swa_sinks.astype(F32), norm0_mix=row(norm0_mix), w_in0=w_in0.astype(BF16), w_out0=w_out0.astype(BF16),
        ret_gn=row(ret_gn), qnorm=row(jnp.tile(swa_qnorm, SWA_HEADS)), knorm=row(jnp.tile(swa_knorm, SWA_KV_HEADS)),
        norm1_mix=row(norm1_mix), w_in1=w_in1.astype(BF16), ssm_a_re=ssm_a_re, ssm_a_im=ssm_a_im,
        ssm_log_dt=ssm_log_dt, ssm_b_re=ssm_b_re, ssm_b_im=ssm_b_im, ssm_c_re=ssm_c_re, ssm_c_im=ssm_c_im,
        ssm_d=row(ssm_d), ssm_w_glu=ssm_w_glu.astype(BF16), conv_w=conv_w.astype(F32), conv_b=row(conv_b),
        conv_ln_g=row(conv_ln_g), conv_ln_b=row(conv_ln_b), w_out1=w_out1.astype(BF16),
    )
    ffn0 = (row(norm0_ffn), ffn0_w_gate.astype(BF16), ffn0_w_up.astype(BF16), ffn0_w_down.astype(BF16))
    moe = (row(norm1_ffn), r_hi, (r_pad - r_hi.astype(F32)).astype(BF16),
           moe_w_gate.astype(BF16), moe_w_up.astype(BF16), moe_w_down.astype(BF16))

    pos_p = jnp.arange(lp, dtype=jnp.int32)
    pos_s = PAST_LEN + jnp.arange(ls, dtype=jnp.int32)
    zeros_kv = jnp.zeros((bp, WINDOW, LANES), F32)
    zeros_ret = jnp.zeros((bp, RET_HEADS, RET_DK, RET_DV), F32)
    zeros_ssm = jnp.zeros((bp, SSM_COLS), F32)
    zeros_conv = jnp.zeros((bp, CONV_K - 1, CONV_WIDTH), F32)
    win = cache_k_swa.shape[1]

    h_p, k_p, v_p, ret_p = _even_mixer(x_prompt, pos_p, zeros_kv, zeros_kv, zeros_ret, True, p)
    h_s, k_s, v_s, ret_s = _even_mixer(
        x_sample, pos_s, cache_k_swa.reshape(bs, win, LANES), cache_v_swa.reshape(bs, win, LANES), state_ret,
        False, p)

    h_p = _dense_ffn(h_p.reshape(bp * lp, d), *ffn0, tm=1024, tf=1408).reshape(bp, lp, d)
    h_s = _dense_ffn(h_s.reshape(bs * ls, d), *ffn0, tm=1024, tf=1408).reshape(bs, ls, d)

    h_p, sre_p, sim_p, conv_p = _odd_mixer(h_p, zeros_ssm, zeros_ssm, zeros_conv, p)
    h_s, sre_s, sim_s, conv_s = _odd_mixer(
        h_s, state_ssm_re.reshape(bs, SSM_COLS), state_ssm_im.reshape(bs, SSM_COLS), state_conv, p)

    h_p = _moe_ffn(h_p.reshape(bp * lp, d), *moe, tm=1024, tf=512).reshape(bp, lp, d)
    h_s = _moe_ffn(h_s.reshape(bs * ls, d), *moe, tm=1024, tf=512).reshape(bs, ls, d)

    kv_p = (bp, WINDOW, SWA_KV_HEADS, SWA_DH)
    kv_s = (bs, win, SWA_KV_HEADS, SWA_DH)
    return (h_p, h_s, k_p.reshape(kv_p), v_p.reshape(kv_p), k_s.reshape(kv_s), v_s.reshape(kv_s), ret_p, ret_s,
            sre_p, sim_p, sre_s, sim_s, conv_p, conv_s)
```

```python
import functools
import math

import jax
import jax.numpy as jnp
import numpy as np
from jax import lax
from jax.experimental import pallas as pl
from jax.experimental.pallas import tpu as pltpu

F32 = jnp.float32
BF16 = jnp.bfloat16

D_MODEL = 1024
CHUNK = 64
EPS = 1e-6
ROPE_THETA = 10000.0
PAST_LEN = 4096
RET_HEADS, RET_DK, RET_DV = 4, 64, 128
SWA_HEADS, SWA_KV_HEADS, SWA_DH = 8, 2, 64
SWA_GROUP = SWA_HEADS // SWA_KV_HEADS
WINDOW = 128
SSM_WIDTH = D_MODEL // 2
SSM_GROUP = 16
SSM_GROUPS = SSM_WIDTH // SSM_GROUP
SSM_STATE = 64
SSM_COLS = SSM_GROUPS * SSM_STATE
CONV_WIDTH = D_MODEL // 2
CONV_K = 31
CONV_PAD = 32
N_EXPERTS = 8
TOP_K = 2

QA0, KA0, VA0, GA0 = 0, 256, 512, 1024
QB0, KB0, VB0 = 1536, 2048, 2176
IN0_COLS = 2304
OUT0_COLS = 1024
IN1_COLS = 1536

LANES = 128
VMEM_LIMIT = 56 * 1024 * 1024


def _rms_norm(x, g):
    return x * lax.rsqrt(jnp.mean(x * x, axis=-1, keepdims=True) + EPS) * g


def _silu(x):
    return x * jax.nn.sigmoid(x)


def _dot(a, b):
    return jnp.dot(a, b, preferred_element_type=F32)


def _dot_nt(a, b):
    return lax.dot_general(a, b, (((1,), (1,)), ((), ())), preferred_element_type=F32)


def _dot_tn(a, b):
    return lax.dot_general(a, b, (((0,), (0,)), ((), ())), preferred_element_type=F32)


def _rot_half(x):
    w = x.shape[-1]
    lane = lax.broadcasted_iota(jnp.int32, x.shape, x.ndim - 1)
    fwd = pltpu.roll(x, w - SWA_DH // 2, x.ndim - 1)
    bwd = pltpu.roll(x, SWA_DH // 2, x.ndim - 1)
    return jnp.where(lane % SWA_DH < SWA_DH // 2, fwd, bwd)


def _rope(x, cos, sin_signed):
    reps = x.shape[-1] // LANES
    if reps > 1:
        cos = jnp.concatenate([cos] * reps, axis=-1)
        sin_signed = jnp.concatenate([sin_signed] * reps, axis=-1)
    return x * cos + _rot_half(x) * sin_signed


def _split_dot(x, w_bf16):
    hi = x.astype(BF16)
    lo = (x - hi.astype(F32)).astype(BF16)
    return _dot(hi, w_bf16) + _dot(lo, w_bf16)


def _even_kernel(sink_ref, x_ref, g_ref, win_ref, wout_ref, cos_ref, sin_ref, intra_ref, qdec_ref, kdec_ref,
                 gn_ref, qn_ref, kn_ref, ones_ref, kprev_ref, vprev_ref, s0_ref,
                 o_ref, knew_ref, vnew_ref, sfin_ref,
                 z_ref, mix_ref, kwin_ref, vwin_ref, s_ref, *, chunks, mask_initial, chunk_decay):
    t = pl.program_id(1)

    @pl.when(t == 0)
    def _():
        kwin_ref[0:WINDOW, :] = kprev_ref[...]
        vwin_ref[0:WINDOW, :] = vprev_ref[...]
        s_ref[...] = s0_ref[...]

    x = x_ref[...]
    xn = _rms_norm(x, g_ref[...]).astype(BF16)
    z_ref[...] = _dot(xn, win_ref[...])

    cos = cos_ref[...]
    sin = sin_ref[...]
    z_ref[:, QA0:KA0] = _rope(z_ref[:, QA0:KA0], cos, sin)
    z_ref[:, KA0:VA0] = _rope(z_ref[:, KA0:VA0], cos, sin) * (RET_DK ** -0.5)
    qk = z_ref[:, QB0:VB0]
    ms = _split_dot(qk * qk, ones_ref[...]) * (1.0 / SWA_DH)
    qk = qk * lax.rsqrt(ms + EPS)
    nq = SWA_HEADS * SWA_DH
    z_ref[:, QB0:KB0] = _rope(qk[:, :nq] * qn_ref[...], cos, sin)
    z_ref[:, KB0:VB0] = _rope(qk[:, nq:] * kn_ref[...], cos, sin)

    row = lax.broadcasted_iota(jnp.int32, (SWA_GROUP * CHUNK, 1), 0)
    key_pos = lax.broadcasted_iota(jnp.int32, (1, WINDOW + CHUNK), 1)

    def chunk_step(c, carry):
        r0 = pl.multiple_of(c * CHUNK, CHUNK)
        rows = pl.ds(r0, CHUNK)
        for h in range(RET_HEADS):
            q = z_ref[rows, QA0 + h * RET_DK:QA0 + (h + 1) * RET_DK]
            k = z_ref[rows, KA0 + h * RET_DK:KA0 + (h + 1) * RET_DK]
            v = z_ref[rows, VA0 + h * RET_DV:VA0 + (h + 1) * RET_DV].astype(BF16)
            gate = z_ref[rows, GA0 + h * RET_DV:GA0 + (h + 1) * RET_DV]
            qd = q * qdec_ref[:, h * RET_DK:(h + 1) * RET_DK]
            kd = k * kdec_ref[:, h * RET_DK:(h + 1) * RET_DK]
            scores = _dot_nt(q.astype(BF16), k.astype(BF16)) * intra_ref[h]
            state = s_ref[h]
            o = _dot(scores.astype(BF16), v) + _dot(qd.astype(BF16), state.astype(BF16))
            s_ref[h] = chunk_decay[h] * state + _dot_tn(kd.astype(BF16), v)
            oc = o - jnp.mean(o, axis=-1, keepdims=True)
            on = oc * lax.rsqrt(jnp.mean(oc * oc, axis=-1, keepdims=True) + EPS)
            on = on * gn_ref[:, h * RET_DV:(h + 1) * RET_DV]
            mix_ref[rows, h * RET_DV:(h + 1) * RET_DV] = (_silu(gate) * on).astype(BF16)
        kwin_ref[WINDOW:, :] = z_ref[rows, KB0:VB0]
        vwin_ref[WINDOW:, :] = z_ref[rows, VB0:IN0_COLS]
        if mask_initial:
            first_valid = WINDOW - (t * chunks + c) * CHUNK
        for j in range(SWA_KV_HEADS):
            keys = kwin_ref[:, j * SWA_DH:(j + 1) * SWA_DH].astype(BF16)
            vals = vwin_ref[:, j * SWA_DH:(j + 1) * SWA_DH].astype(BF16)
            heads = [SWA_GROUP * j + g for g in range(SWA_GROUP)]
            qg = jnp.concatenate(
                [z_ref[rows, QB0 + hh * SWA_DH:QB0 + (hh + 1) * SWA_DH] for hh in heads], axis=0)
            s = _dot_nt(qg.astype(BF16), keys) * (SWA_DH ** -0.5)
            if mask_initial:
                s = jnp.where(key_pos >= first_valid, s, -1e30)
            sink = jnp.zeros((SWA_GROUP * CHUNK, 1), F32)
            for g, hh in enumerate(heads):
                sink = jnp.where(row // CHUNK == g, sink_ref[hh], sink)
            m = jnp.maximum(jnp.max(s, axis=-1, keepdims=True), sink)
            p = jnp.exp(s - m)
            den = jnp.sum(p, axis=-1, keepdims=True) + jnp.exp(sink - m)
            o = _dot(p.astype(BF16), vals) / den
            for g, hh in enumerate(heads):
                c0 = RET_HEADS * RET_DV + hh * SWA_DH
                mix_ref[rows, c0:c0 + SWA_DH] = o[g * CHUNK:(g + 1) * CHUNK, :].astype(BF16)
        kwin_ref[0:WINDOW, :] = kwin_ref[CHUNK:, :]
        vwin_ref[0:WINDOW, :] = vwin_ref[CHUNK:, :]
        return carry

    lax.fori_loop(0, chunks, chunk_step, 0)

    o_ref[...] = x + _dot(mix_ref[...], wout_ref[...])
    knew_ref[...] = kwin_ref[0:WINDOW, :]
    vnew_ref[...] = vwin_ref[0:WINDOW, :]
    sfin_ref[...] = s_ref[...]


def _retention_tables():
    h = np.arange(RET_HEADS, dtype=np.float64)
    log_g = np.log1p(-np.exp2(-5.0 - h))
    idx = np.arange(CHUNK, dtype=np.float64)
    intra = np.exp(np.abs(idx[:, None] - idx[None, :])[None] * log_g[:, None, None])
    q_dec = np.exp((idx + 1.0)[:, None] * log_g[None, :])
    k_dec = np.exp((CHUNK - 1.0 - idx)[:, None] * log_g[None, :])
    chunk_dec = tuple(float(v) for v in np.exp(CHUNK * log_g))
    q_dec = np.repeat(q_dec, RET_DK, axis=1)
    k_dec = np.repeat(k_dec, RET_DK, axis=1)
    return (jnp.asarray(intra, F32), jnp.asarray(q_dec, F32), jnp.asarray(k_dec, F32), chunk_dec)


def _rope_tables(pos):
    half = SWA_DH // 2
    inv = jnp.exp(-math.log(ROPE_THETA) * jnp.arange(half, dtype=F32) / half)
    ang = pos.astype(F32)[:, None] * inv[None, :]
    cos, sin = jnp.cos(ang), jnp.sin(ang)
    cos = jnp.concatenate([cos, cos] * (LANES // SWA_DH), axis=-1)
    sin = jnp.concatenate([-sin, sin] * (LANES // SWA_DH), axis=-1)
    return cos, sin


def _even_mixer(x, pos, kprev, vprev, s0, mask_initial, p):
    b, l, d = x.shape
    tq = min(512, l)
    chunks = tq // CHUNK
    intra, q_dec, k_dec, chunk_decay = _retention_tables()
    cos, sin = _rope_tables(pos)
    nqk = (SWA_HEADS + SWA_KV_HEADS) * SWA_DH
    group = np.arange(nqk) // SWA_DH
    ones_blk = jnp.asarray(group[:, None] == group[None, :], BF16)

    def const(shape):
        return pl.BlockSpec(shape, lambda bi, ti: (0,) * len(shape))

    kv_spec = pl.BlockSpec((None, WINDOW, LANES), lambda bi, ti: (bi, 0, 0))
    st_spec = pl.BlockSpec((None, RET_HEADS, RET_DK, RET_DV), lambda bi, ti: (bi, 0, 0, 0))
    x_spec = pl.BlockSpec((None, tq, d), lambda bi, ti: (bi, ti, 0))
    tab_spec = pl.BlockSpec((tq, LANES), lambda bi, ti: (ti, 0))
    kern = functools.partial(_even_kernel, chunks=chunks, mask_initial=mask_initial, chunk_decay=chunk_decay)
    return pl.pallas_call(
        kern,
        grid=(b, l // tq),
        in_specs=[
            pl.BlockSpec(memory_space=pltpu.SMEM),
            x_spec, const((1, d)), const((d, IN0_COLS)), const((OUT0_COLS, d)),
            tab_spec, tab_spec,
            const((RET_HEADS, CHUNK, CHUNK)), const((CHUNK, RET_HEADS * RET_DK)), const((CHUNK, RET_HEADS * RET_DK)),
            const((1, RET_HEADS * RET_DV)), const((1, SWA_HEADS * SWA_DH)), const((1, SWA_KV_HEADS * SWA_DH)),
            const((nqk, nqk)), kv_spec, kv_spec, st_spec,
        ],
        out_specs=[x_spec, kv_spec, kv_spec, st_spec],
        out_shape=[
            jax.ShapeDtypeStruct((b, l, d), F32),
            jax.ShapeDtypeStruct((b, WINDOW, LANES), F32),
            jax.ShapeDtypeStruct((b, WINDOW, LANES), F32),
            jax.ShapeDtypeStruct((b, RET_HEADS, RET_DK, RET_DV), F32),
        ],
        scratch_shapes=[
            pltpu.VMEM((tq, IN0_COLS), F32),
            pltpu.VMEM((tq, OUT0_COLS), BF16),
            pltpu.VMEM((WINDOW + CHUNK, LANES), F32),
            pltpu.VMEM((WINDOW + CHUNK, LANES), F32),
            pltpu.VMEM((RET_HEADS, RET_DK, RET_DV), F32),
        ],
        compiler_params=pltpu.CompilerParams(
            dimension_semantics=("arbitrary", "arbitrary"), vmem_limit_bytes=VMEM_LIMIT),
        name="even_mixer",
    )(p["sinks"], x, p["norm0_mix"], p["w_in0"], p["w_out0"], cos, sin, intra, q_dec, k_dec,
      p["ret_gn"], p["qnorm"], p["knorm"], ones_blk, kprev, vprev, s0)


SSM_SLABS = SSM_COLS // LANES
SCAN_SLABS = 4
BATCH_ROWS = 8


def _odd_kernel(x_ref, g_ref, win_ref, bbd_ref, lre_ref, lim_ref, cbd_ref, d_ref, wglu_ref, cw_ref, cb_ref,
                lng_ref, lnb_ref, wout_ref, s0re_ref, s0im_ref, conv0_ref,
                o_ref, sre_ref, sim_ref, convn_ref,
                bu_ref, hre_ref, him_ref, xp_ref, *, tm):
    t = pl.program_id(1)
    rows = BATCH_ROWS * tm

    @pl.when(t == 0)
    def _():
        for j in range(SSM_SLABS):
            hre_ref[j] = s0re_ref[:, j * LANES:(j + 1) * LANES]
            him_ref[j] = s0im_ref[:, j * LANES:(j + 1) * LANES]
        xp_ref[:, 0:CONV_PAD, :] = conv0_ref[...]

    x = x_ref[...].reshape(rows, D_MODEL)
    xn = _rms_norm(x, g_ref[...]).astype(BF16)
    z = _dot(xn, win_ref[...])
    u = z[:, 0:SSM_WIDTH]
    d_val = z[:, SSM_WIDTH:SSM_WIDTH + CONV_WIDTH]
    d_gate = z[:, SSM_WIDTH + CONV_WIDTH:]

    bu = _dot(u.astype(BF16), bbd_ref[...])
    for j in range(2 * SSM_SLABS):
        bu_ref[j] = bu[:, j * LANES:(j + 1) * LANES]
    for j0 in range(0, SSM_SLABS, SCAN_SLABS):
        slabs = range(j0, j0 + SCAN_SLABS)
        lr = [lre_ref[j] for j in slabs]
        li = [lim_ref[j] for j in slabs]

        def step(ti, carry):
            at_t = pl.ds(ti, BATCH_ROWS, stride=tm)
            out = []
            for n, j in enumerate(slabs):
                hr, hi = carry[n]
                nr = lr[n] * hr - li[n] * hi + bu_ref[j, at_t, :]
                ni = lr[n] * hi + li[n] * hr + bu_ref[SSM_SLABS + j, at_t, :]
                bu_ref[j, at_t, :] = nr
                bu_ref[SSM_SLABS + j, at_t, :] = ni
                out.append((nr, ni))
            return tuple(out)

        init = tuple((hre_ref[j], him_ref[j]) for j in slabs)
        final = lax.fori_loop(0, tm, step, init, unroll=4)
        for n, j in enumerate(slabs):
            hre_ref[j], him_ref[j] = final[n]

    states = jnp.concatenate([bu_ref[j].astype(BF16) for j in range(2 * SSM_SLABS)], axis=-1)
    y = _dot(states, cbd_ref[...]) + d_ref[...] * u
    g = 0.5 * y * (1.0 + lax.erf(y * (2.0 ** -0.5)))
    o_c = g * jax.nn.sigmoid(_dot(g.astype(BF16), wglu_ref[...]))

    u_d = d_val * jax.nn.sigmoid(d_gate)
    xp_ref[:, CONV_PAD:, :] = u_d.reshape(BATCH_ROWS, tm, CONV_WIDTH)
    first = CONV_PAD - (CONV_K - 1)
    acc = jnp.zeros((BATCH_ROWS, tm, CONV_WIDTH), F32) + cb_ref[...]
    for k in range(CONV_K):
        acc = acc + xp_ref[:, first + k:first + k + tm, :] * cw_ref[k:k + 1, :]
    mu = jnp.mean(acc, axis=-1, keepdims=True)
    xc = acc - mu
    var = jnp.mean(xc * xc, axis=-1, keepdims=True)
    o_d = _silu(xc * lax.rsqrt(var + EPS) * lng_ref[...] + lnb_ref[...]).reshape(rows, CONV_WIDTH)
    xp_ref[:, 0:CONV_PAD, :] = xp_ref[:, tm:tm + CONV_PAD, :]

    mix = jnp.concatenate([o_c.astype(BF16), o_d.astype(BF16)], axis=-1)
    o_ref[...] = (x + _dot(mix, wout_ref[...])).reshape(BATCH_ROWS, tm, D_MODEL)
    for j in range(SSM_SLABS):
        sre_ref[:, j * LANES:(j + 1) * LANES] = hre_ref[j]
        sim_ref[:, j * LANES:(j + 1) * LANES] = him_ref[j]
    convn_ref[...] = xp_ref[:, 0:CONV_PAD, :]


def _s5_discretize(p):
    dt = jnp.exp(p["ssm_log_dt"].astype(F32))[:, None]
    are, aim = p["ssm_a_re"].astype(F32), p["ssm_a_im"].astype(F32)
    mag = jnp.exp(are * dt)
    lb_re, lb_im = mag * jnp.cos(aim * dt), mag * jnp.sin(aim * dt)
    den = are * are + aim * aim
    nr, ni = lb_re - 1.0, lb_im
    f_re = (nr * are + ni * aim) / den
    f_im = (ni * are - nr * aim) / den
    br, bi = p["ssm_b_re"].astype(F32), p["ssm_b_im"].astype(F32)
    bb_re = f_re[..., None] * br - f_im[..., None] * bi
    bb_im = f_re[..., None] * bi + f_im[..., None] * br
    return lb_re, lb_im, bb_re, bb_im


def _s5_operands(p):
    lb_re, lb_im, bb_re, bb_im = _s5_discretize(p)
    eye = jnp.eye(SSM_GROUPS, dtype=F32)

    def in_map(bb):
        return jnp.einsum("gpc,gh->gchp", bb, eye).reshape(SSM_WIDTH, SSM_COLS)

    def out_map(c):
        return jnp.einsum("gcp,gh->gphc", c.astype(F32), eye).reshape(SSM_COLS, SSM_WIDTH)

    bbd = jnp.concatenate([in_map(bb_re), in_map(bb_im)], axis=1).astype(BF16)
    cbd = jnp.concatenate([out_map(p["ssm_c_re"]), -out_map(p["ssm_c_im"])], axis=0).astype(BF16)
    slab_shape = (SSM_SLABS, BATCH_ROWS, LANES)
    lre = jnp.broadcast_to(lb_re.reshape(SSM_SLABS, 1, LANES), slab_shape)
    lim = jnp.broadcast_to(lb_im.reshape(SSM_SLABS, 1, LANES), slab_shape)
    return bbd, cbd, lre, lim


def _odd_mixer(x, s0re, s0im, conv0, p):
    b, l, d = x.shape
    tm = min(64, l)
    bbd, cbd, lre, lim = _s5_operands(p)
    conv0 = jnp.pad(conv0, ((0, 0), (CONV_PAD - (CONV_K - 1), 0), (0, 0)))
    conv_w = jnp.pad(p["conv_w"], ((0, CONV_PAD - CONV_K), (0, 0)))

    def const(shape):
        return pl.BlockSpec(shape, lambda bi, ti: (0,) * len(shape))

    x_spec = pl.BlockSpec((BATCH_ROWS, tm, d), lambda bi, ti: (bi, ti, 0))
    st_spec = pl.BlockSpec((BATCH_ROWS, SSM_COLS), lambda bi, ti: (bi, 0))
    cv_spec = pl.BlockSpec((BATCH_ROWS, CONV_PAD, CONV_WIDTH), lambda bi, ti: (bi, 0, 0))
    out, sre, sim, convn = pl.pallas_call(
        functools.partial(_odd_kernel, tm=tm),
        grid=(b // BATCH_ROWS, l // tm),
        in_specs=[
            x_spec, const((1, d)), const((d, IN1_COLS)), const((SSM_WIDTH, 2 * SSM_COLS)),
            const((SSM_SLABS, BATCH_ROWS, LANES)), const((SSM_SLABS, BATCH_ROWS, LANES)),
            const((2 * SSM_COLS, SSM_WIDTH)),
            const((1, SSM_WIDTH)), const((SSM_WIDTH, SSM_WIDTH)), const((CONV_PAD, CONV_WIDTH)),
            const((1, CONV_WIDTH)), const((1, CONV_WIDTH)), const((1, CONV_WIDTH)),
            const((SSM_WIDTH + CONV_WIDTH, d)), st_spec, st_spec, cv_spec,
        ],
        out_specs=[x_spec, st_spec, st_spec, cv_spec],
        out_shape=[
            jax.ShapeDtypeStruct((b, l, d), F32),
            jax.ShapeDtypeStruct((b, SSM_COLS), F32),
            jax.ShapeDtypeStruct((b, SSM_COLS), F32),
            jax.ShapeDtypeStruct((b, CONV_PAD, CONV_WIDTH), F32),
        ],
        scratch_shapes=[
            pltpu.VMEM((2 * SSM_SLABS, BATCH_ROWS * tm, LANES), F32),
            pltpu.VMEM((SSM_SLABS, BATCH_ROWS, LANES), F32),
            pltpu.VMEM((SSM_SLABS, BATCH_ROWS, LANES), F32),
            pltpu.VMEM((BATCH_ROWS, CONV_PAD + tm, CONV_WIDTH), F32),
        ],
        compiler_params=pltpu.CompilerParams(
            dimension_semantics=("arbitrary", "arbitrary"), vmem_limit_bytes=VMEM_LIMIT),
        name="odd_mixer",
    )(x, p["norm1_mix"], p["w_in1"], bbd, lre, lim, cbd, p["ssm_d"], p["ssm_w_glu"], conv_w, p["conv_b"],
      p["conv_ln_g"], p["conv_ln_b"], p["w_out1"], s0re, s0im, conv0)
    shape3 = (b, SSM_GROUPS, SSM_STATE)
    return out, sre.reshape(shape3), sim.reshape(shape3), convn[:, CONV_PAD - (CONV_K - 1):, :]


def _ffn_kernel(x_ref, g_ref, wg_ref, wu_ref, wd_ref, o_ref, xn_ref):
    j = pl.program_id(1)

    @pl.when(j == 0)
    def _():
        x = x_ref[...]
        xn_ref[...] = _rms_norm(x, g_ref[...]).astype(BF16)
        o_ref[...] = x

    xn = xn_ref[...]
    hidden = _silu(_dot(xn, wg_ref[...])) * _dot(xn, wu_ref[...])
    o_ref[...] += _dot(hidden.astype(BF16), wd_ref[...])


def _dense_ffn(x, g, wg, wu, wd, *, tm, tf):
    n, d = x.shape
    f = wg.shape[1]
    x_spec = pl.BlockSpec((tm, d), lambda i, j: (i, 0))
    return pl.pallas_call(
        _ffn_kernel,
        grid=(n // tm, f // tf),
        in_specs=[
            x_spec, pl.BlockSpec((1, d), lambda i, j: (0, 0)),
            pl.BlockSpec((d, tf), lambda i, j: (0, j)), pl.BlockSpec((d, tf), lambda i, j: (0, j)),
            pl.BlockSpec((tf, d), lambda i, j: (j, 0)),
        ],
        out_specs=x_spec,
        out_shape=jax.ShapeDtypeStruct((n, d), F32),
        scratch_shapes=[pltpu.VMEM((tm, d), BF16)],
        compiler_params=pltpu.CompilerParams(
            dimension_semantics=("arbitrary", "arbitrary"), vmem_limit_bytes=VMEM_LIMIT),
        name="dense_ffn",
    )(x, g, wg, wu, wd)


ROUTE_GATE_LANE = 0
ROUTE_INDEX_LANE = 2


def _route_kernel(x_ref, g_ref, r_hi_ref, r_lo_ref, o_ref):
    xn = _rms_norm(x_ref[...], g_ref[...])
    logits = _split_dot(xn, r_hi_ref[...]) + _dot(xn.astype(BF16), r_lo_ref[...])
    lane = lax.broadcasted_iota(jnp.int32, logits.shape, 1)
    logits = jnp.where(lane < N_EXPERTS, logits, -jnp.inf)
    v1 = jnp.max(logits, axis=-1, keepdims=True)
    i1 = jnp.min(jnp.where(logits == v1, lane, LANES), axis=-1, keepdims=True)
    rest = jnp.where(lane == i1, -jnp.inf, logits)
    v2 = jnp.max(rest, axis=-1, keepdims=True)
    i2 = jnp.min(jnp.where(rest == v2, lane, LANES), axis=-1, keepdims=True)
    e2 = jnp.exp(v2 - v1)
    den = 1.0 + e2
    rec = jnp.where(lane == ROUTE_GATE_LANE, 1.0 / den, 0.0)
    rec = jnp.where(lane == ROUTE_GATE_LANE + 1, e2 / den, rec)
    rec = jnp.where(lane == ROUTE_INDEX_LANE, i1.astype(F32), rec)
    rec = jnp.where(lane == ROUTE_INDEX_LANE + 1, i2.astype(F32), rec)
    o_ref[...] = rec


def _gather_rows(src_hbm, dst_ref, index_ref, sem, count):
    def issue(r, carry):
        pltpu.make_async_copy(src_hbm.at[pl.ds(index_ref[r], 1), :], dst_ref.at[pl.ds(r, 1), :], sem).start()
        return carry

    lax.fori_loop(0, count, issue, 0)
    pltpu.make_async_copy(src_hbm.at[pl.ds(0, count), :], dst_ref, sem).wait()


def _expert_kernel(texp_ref, used_ref, src_ref, x_hbm, g_ref, wg_ref, wu_ref, wd_ref, y_ref, xg_ref, xn_ref, sem,
                   *, tm):
    i = pl.program_id(0)
    j = pl.program_id(1)
    used = used_ref[i] > 0

    @pl.when(jnp.logical_and(j == 0, used))
    def _():
        _gather_rows(x_hbm, xg_ref, src_ref, sem, tm)
        xn_ref[...] = _rms_norm(xg_ref[...], g_ref[...]).astype(BF16)

    @pl.when(jnp.logical_and(j == 0, jnp.logical_not(used)))
    def _():
        y_ref[...] = jnp.zeros_like(y_ref)

    @pl.when(used)
    def _():
        xn = xn_ref[...]
        hidden = _silu(_dot(xn, wg_ref[...])) * _dot(xn, wu_ref[...])
        part = _dot(hidden.astype(BF16), wd_ref[...])

        @pl.when(j == 0)
        def _():
            y_ref[...] = part

        @pl.when(j > 0)
        def _():
            y_ref[...] += part


def _combine_kernel(pos0_ref, pos1_ref, route_ref, x_ref, y_hbm, o_ref, y0_ref, y1_ref, sem, *, tc):
    _gather_rows(y_hbm, y0_ref, pos0_ref, sem.at[0], tc)
    _gather_rows(y_hbm, y1_ref, pos1_ref, sem.at[1], tc)
    route = route_ref[...]
    g0 = route[:, ROUTE_GATE_LANE:ROUTE_GATE_LANE + 1]
    g1 = route[:, ROUTE_GATE_LANE + 1:ROUTE_GATE_LANE + 2]
    o_ref[...] = x_ref[...] + (g0 * y0_ref[...] + g1 * y1_ref[...])


def _dispatch_plan(experts, tm, n_tiles):
    n = experts.shape[0]
    flat = experts.reshape(-1)
    onehot = (flat[:, None] == jnp.arange(N_EXPERTS, dtype=jnp.int32)[None, :]).astype(jnp.int32)
    csum = jnp.cumsum(onehot, axis=0)
    rank = jnp.sum((csum - onehot) * onehot, axis=1)
    counts = csum[-1]
    tiles = (counts + tm - 1) // tm
    tile_end = jnp.cumsum(tiles)
    tile_start = tile_end - tiles
    pos = tile_start[flat] * tm + rank
    src = jnp.zeros((n_tiles * tm,), jnp.int32).at[pos].set(jnp.arange(2 * n, dtype=jnp.int32) // 2)
    tile_ids = jnp.arange(n_tiles, dtype=jnp.int32)
    used = (tile_ids < tile_end[-1]).astype(jnp.int32)
    tile_expert = jnp.minimum(jnp.searchsorted(tile_end, tile_ids, side="right"), N_EXPERTS - 1).astype(jnp.int32)
    last_expert = tile_expert[jnp.maximum(tile_end[-1] - 1, 0)]
    tile_expert = jnp.where(used > 0, tile_expert, last_expert)
    pos = pos.reshape(n, 2)
    return src, pos[:, 0], pos[:, 1], tile_expert, used


def _moe_ffn(x, g, r_hi, r_lo, wg, wu, wd, *, tr, tm, tf, tc):
    n, d = x.shape
    e, _, f = wg.shape
    tpe = f // tf
    n_tiles = (TOP_K * n) // tm + e
    smem = pltpu.SMEM

    route = pl.pallas_call(
        _route_kernel,
        grid=(n // tr,),
        in_specs=[
            pl.BlockSpec((tr, d), lambda i: (i, 0)), pl.BlockSpec((1, d), lambda i: (0, 0)),
            pl.BlockSpec((d, LANES), lambda i: (0, 0)), pl.BlockSpec((d, LANES), lambda i: (0, 0)),
        ],
        out_specs=pl.BlockSpec((tr, LANES), lambda i: (i, 0)),
        out_shape=jax.ShapeDtypeStruct((n, LANES), F32),
        compiler_params=pltpu.CompilerParams(dimension_semantics=("arbitrary",), vmem_limit_bytes=VMEM_LIMIT),
        name="moe_route",
    )(x, g, r_hi, r_lo)

    experts = route[:, ROUTE_INDEX_LANE:ROUTE_INDEX_LANE + TOP_K].astype(jnp.int32)
    src, pos0, pos1, tile_expert, used = _dispatch_plan(experts, tm, n_tiles)

    def ff_tile(i, j, texp, used):
        return jnp.where(used[i] > 0, j, tpe - 1)

    y = pl.pallas_call(
        functools.partial(_expert_kernel, tm=tm),
        grid_spec=pltpu.PrefetchScalarGridSpec(
            num_scalar_prefetch=2,
            grid=(n_tiles, tpe),
            in_specs=[
                pl.BlockSpec((tm,), lambda i, j, texp, used: (i,), memory_space=smem),
                pl.BlockSpec(memory_space=pl.ANY),
                pl.BlockSpec((1, d), lambda i, j, texp, used: (0, 0)),
                pl.BlockSpec((None, d, tf), lambda i, j, texp, used: (texp[i], 0, ff_tile(i, j, texp, used))),
                pl.BlockSpec((None, d, tf), lambda i, j, texp, used: (texp[i], 0, ff_tile(i, j, texp, used))),
                pl.BlockSpec((None, tf, d), lambda i, j, texp, used: (texp[i], ff_tile(i, j, texp, used), 0)),
            ],
            out_specs=pl.BlockSpec((tm, d), lambda i, j, texp, used: (i, 0)),
            scratch_shapes=[pltpu.VMEM((tm, d), F32), pltpu.VMEM((tm, d), BF16), pltpu.SemaphoreType.DMA(())],
        ),
        out_shape=jax.ShapeDtypeStruct((n_tiles * tm, d), F32),
        compiler_params=pltpu.CompilerParams(
            dimension_semantics=("arbitrary", "arbitrary"), vmem_limit_bytes=VMEM_LIMIT),
        name="moe_experts",
    )(tile_expert, used, src, x, g, wg, wu, wd)

    return pl.pallas_call(
        functools.partial(_combine_kernel, tc=tc),
        grid=(n // tc,),
        in_specs=[
            pl.BlockSpec((tc,), lambda i: (i,), memory_space=smem),
            pl.BlockSpec((tc,), lambda i: (i,), memory_space=smem),
            pl.BlockSpec((tc, LANES), lambda i: (i, 0)),
            pl.BlockSpec((tc, d), lambda i: (i, 0)),
            pl.BlockSpec(memory_space=pl.ANY),
        ],
        out_specs=pl.BlockSpec((tc, d), lambda i: (i, 0)),
        out_shape=jax.ShapeDtypeStruct((n, d), F32),
        scratch_shapes=[pltpu.VMEM((tc, d), F32), pltpu.VMEM((tc, d), F32), pltpu.SemaphoreType.DMA((2,))],
        compiler_params=pltpu.CompilerParams(dimension_semantics=("arbitrary",), vmem_limit_bytes=VMEM_LIMIT),
        name="moe_combine",
    )(pos0, pos1, route, x, y)


def kernel(x_prompt, x_sample, cache_k_swa, cache_v_swa, state_ret, state_ssm_re, state_ssm_im, state_conv,
           norm0_mix, w_in0, ret_gn, swa_qnorm, swa_knorm, swa_sinks, w_out0, norm0_ffn, ffn0_w_gate,
           ffn0_w_up, ffn0_w_down, norm1_mix, w_in1, ssm_a_re, ssm_a_im, ssm_log_dt, ssm_b_re, ssm_b_im,
           ssm_c_re, ssm_c_im, ssm_d, ssm_w_glu, conv_w, conv_b, conv_ln_g, conv_ln_b, w_out1, norm1_ffn,
           moe_router, moe_w_gate, moe_w_up, moe_w_down):
    bp, lp, d = x_prompt.shape
    bs, ls, _ = x_sample.shape
    row = lambda v: v.astype(F32).reshape(1, -1)
    r_pad = jnp.pad(moe_router.astype(F32), ((0, 0), (0, LANES - N_EXPERTS)))
    r_hi = r_pad.astype(BF16)
    p = dict(
        sinks=swa_sinks.astype(F32), norm0_mix=row(norm0_mix), w_in0=w_in0.astype(BF16), w_out0=w_out0.astype(BF16),
        ret_gn=row(ret_gn), qnorm=row(jnp.tile(swa_qnorm, SWA_HEADS)), knorm=row(jnp.tile(swa_knorm, SWA_KV_HEADS)),
        norm1_mix=row(norm1_mix), w_in1=w_in1.astype(BF16), ssm_a_re=ssm_a_re, ssm_a_im=ssm_a_im,
        ssm_log_dt=ssm_log_dt, ssm_b_re=ssm_b_re, ssm_b_im=ssm_b_im, ssm_c_re=ssm_c_re, ssm_c_im=ssm_c_im,
        ssm_d=row(ssm_d), ssm_w_glu=ssm_w_glu.astype(BF16), conv_w=conv_w.astype(F32), conv_b=row(conv_b),
        conv_ln_g=row(conv_ln_g), conv_ln_b=row(conv_ln_b), w_out1=w_out1.astype(BF16),
    )
    ffn0 = (row(norm0_ffn), ffn0_w_gate.astype(BF16), ffn0_w_up.astype(BF16), ffn0_w_down.astype(BF16))
    moe = (row(norm1_ffn), r_hi, (r_pad - r_hi.astype(F32)).astype(BF16),
           moe_w_gate.astype(BF16), moe_w_up.astype(BF16), moe_w_down.astype(BF16))

    pos_p = jnp.arange(lp, dtype=jnp.int32)
    pos_s = PAST_LEN + jnp.arange(ls, dtype=jnp.int32)
    zeros_kv = jnp.zeros((bp, WINDOW, LANES), F32)
    zeros_ret = jnp.zeros((bp, RET_HEADS, RET_DK, RET_DV), F32)
    zeros_ssm = jnp.zeros((bp, SSM_COLS), F32)
    zeros_conv = jnp.zeros((bp, CONV_K - 1, CONV_WIDTH), F32)
    win = cache_k_swa.shape[1]

    h_p, k_p, v_p, ret_p = _even_mixer(x_prompt, pos_p, zeros_kv, zeros_kv, zeros_ret, True, p)
    h_s, k_s, v_s, ret_s = _even_mixer(
        x_sample, pos_s, cache_k_swa.reshape(bs, win, LANES), cache_v_swa.reshape(bs, win, LANES), state_ret,
        False, p)

    h_p = _dense_ffn(h_p.reshape(bp * lp, d), *ffn0, tm=1024, tf=1408).reshape(bp, lp, d)
    h_s = _dense_ffn(h_s.reshape(bs * ls, d), *ffn0, tm=1024, tf=1408).reshape(bs, ls, d)

    h_p, sre_p, sim_p, conv_p = _odd_mixer(h_p, zeros_ssm, zeros_ssm, zeros_conv, p)
    h_s, sre_s, sim_s, conv_s = _odd_mixer(
        h_s, state_ssm_re.reshape(bs, SSM_COLS), state_ssm_im.reshape(bs, SSM_COLS), state_conv, p)

    h_p = _moe_ffn(h_p.reshape(bp * lp, d), *moe, tr=1024, tm=1024, tf=512, tc=512).reshape(bp, lp, d)
    h_s = _moe_ffn(h_s.reshape(bs * ls, d), *moe, tr=1024, tm=256, tf=512, tc=512).reshape(bs, ls, d)

    kv_p = (bp, WINDOW, SWA_KV_HEADS, SWA_DH)
    kv_s = (bs, win, SWA_KV_HEADS, SWA_DH)
    return (h_p, h_s, k_p.reshape(kv_p), v_p.reshape(kv_p), k_s.reshape(kv_s), v_s.reshape(kv_s), ret_p, ret_s,
            sre_p, sim_p, sre_s, sim_s, conv_p, conv_s)
```

```python
import functools
import math

import jax
import jax.numpy as jnp
import numpy as np
from jax import lax
from jax.experimental import pallas as pl
from jax.experimental.pallas import tpu as pltpu

F32 = jnp.float32
BF16 = jnp.bfloat16

D_MODEL = 1024
CHUNK = 64
EPS = 1e-6
ROPE_THETA = 10000.0
PAST_LEN = 4096
RET_HEADS, RET_DK, RET_DV = 4, 64, 128
SWA_HEADS, SWA_KV_HEADS, SWA_DH = 8, 2, 64
SWA_GROUP = SWA_HEADS // SWA_KV_HEADS
WINDOW = 128
SSM_WIDTH = D_MODEL // 2
SSM_GROUP = 16
SSM_GROUPS = SSM_WIDTH // SSM_GROUP
SSM_STATE = 64
SSM_COLS = SSM_GROUPS * SSM_STATE
CONV_WIDTH = D_MODEL // 2
CONV_K = 31
CONV_PAD = 32
N_EXPERTS = 8
TOP_K = 2

QA0, KA0, VA0, GA0 = 0, 256, 512, 1024
QB0, KB0, VB0 = 1536, 2048, 2176
IN0_COLS = 2304
OUT0_COLS = 1024
IN1_COLS = 1536

LANES = 128
VMEM_LIMIT = 56 * 1024 * 1024


def _rms_norm(x, g):
    return x * lax.rsqrt(jnp.mean(x * x, axis=-1, keepdims=True) + EPS) * g


def _silu(x):
    return x * jax.nn.sigmoid(x)


def _dot(a, b):
    return jnp.dot(a, b, preferred_element_type=F32)


def _dot_nt(a, b):
    return lax.dot_general(a, b, (((1,), (1,)), ((), ())), preferred_element_type=F32)


def _dot_tn(a, b):
    return lax.dot_general(a, b, (((0,), (0,)), ((), ())), preferred_element_type=F32)


def _rot_half(x):
    w = x.shape[-1]
    lane = lax.broadcasted_iota(jnp.int32, x.shape, x.ndim - 1)
    fwd = pltpu.roll(x, w - SWA_DH // 2, x.ndim - 1)
    bwd = pltpu.roll(x, SWA_DH // 2, x.ndim - 1)
    return jnp.where(lane % SWA_DH < SWA_DH // 2, fwd, bwd)


def _rope(x, cos, sin_signed):
    reps = x.shape[-1] // LANES
    if reps > 1:
        cos = jnp.concatenate([cos] * reps, axis=-1)
        sin_signed = jnp.concatenate([sin_signed] * reps, axis=-1)
    return x * cos + _rot_half(x) * sin_signed


def _split_dot(x, w_bf16):
    hi = x.astype(BF16)
    lo = (x - hi.astype(F32)).astype(BF16)
    return _dot(hi, w_bf16) + _dot(lo, w_bf16)


def _even_kernel(sink_ref, x_ref, g_ref, win_ref, wout_ref, cos_ref, sin_ref, intra_ref, qdec_ref, kdec_ref,
                 gn_ref, qn_ref, kn_ref, ones_ref, kprev_ref, vprev_ref, s0_ref,
                 o_ref, knew_ref, vnew_ref, sfin_ref,
                 z_ref, mix_ref, kwin_ref, vwin_ref, s_ref, *, chunks, mask_initial, chunk_decay):
    t = pl.program_id(1)

    @pl.when(t == 0)
    def _():
        kwin_ref[0:WINDOW, :] = kprev_ref[...]
        vwin_ref[0:WINDOW, :] = vprev_ref[...]
        s_ref[...] = s0_ref[...]

    x = x_ref[...]
    xn = _rms_norm(x, g_ref[...]).astype(BF16)
    z_ref[...] = _dot(xn, win_ref[...])

    cos = cos_ref[...]
    sin = sin_ref[...]
    z_ref[:, QA0:KA0] = _rope(z_ref[:, QA0:KA0], cos, sin)
    z_ref[:, KA0:VA0] = _rope(z_ref[:, KA0:VA0], cos, sin) * (RET_DK ** -0.5)
    qk = z_ref[:, QB0:VB0]
    ms = _split_dot(qk * qk, ones_ref[...]) * (1.0 / SWA_DH)
    qk = qk * lax.rsqrt(ms + EPS)
    nq = SWA_HEADS * SWA_DH
    z_ref[:, QB0:KB0] = _rope(qk[:, :nq] * qn_ref[...], cos, sin)
    z_ref[:, KB0:VB0] = _rope(qk[:, nq:] * kn_ref[...], cos, sin)

    row = lax.broadcasted_iota(jnp.int32, (SWA_GROUP * CHUNK, 1), 0)
    key_pos = lax.broadcasted_iota(jnp.int32, (1, WINDOW + CHUNK), 1)

    def chunk_step(c, carry):
        r0 = pl.multiple_of(c * CHUNK, CHUNK)
        rows = pl.ds(r0, CHUNK)
        for h in range(RET_HEADS):
            q = z_ref[rows, QA0 + h * RET_DK:QA0 + (h + 1) * RET_DK]
            k = z_ref[rows, KA0 + h * RET_DK:KA0 + (h + 1) * RET_DK]
            v = z_ref[rows, VA0 + h * RET_DV:VA0 + (h + 1) * RET_DV].astype(BF16)
            gate = z_ref[rows, GA0 + h * RET_DV:GA0 + (h + 1) * RET_DV]
            qd = q * qdec_ref[:, h * RET_DK:(h + 1) * RET_DK]
            kd = k * kdec_ref[:, h * RET_DK:(h + 1) * RET_DK]
            scores = _dot_nt(q.astype(BF16), k.astype(BF16)) * intra_ref[h]
            state = s_ref[h]
            o = _dot(scores.astype(BF16), v) + _dot(qd.astype(BF16), state.astype(BF16))
            s_ref[h] = chunk_decay[h] * state + _dot_tn(kd.astype(BF16), v)
            oc = o - jnp.mean(o, axis=-1, keepdims=True)
            on = oc * lax.rsqrt(jnp.mean(oc * oc, axis=-1, keepdims=True) + EPS)
            on = on * gn_ref[:, h * RET_DV:(h + 1) * RET_DV]
            mix_ref[rows, h * RET_DV:(h + 1) * RET_DV] = (_silu(gate) * on).astype(BF16)
        kwin_ref[WINDOW:, :] = z_ref[rows, KB0:VB0]
        vwin_ref[WINDOW:, :] = z_ref[rows, VB0:IN0_COLS]
        if mask_initial:
            first_valid = WINDOW - (t * chunks + c) * CHUNK
        for j in range(SWA_KV_HEADS):
            keys = kwin_ref[:, j * SWA_DH:(j + 1) * SWA_DH].astype(BF16)
            vals = vwin_ref[:, j * SWA_DH:(j + 1) * SWA_DH].astype(BF16)
            heads = [SWA_GROUP * j + g for g in range(SWA_GROUP)]
            qg = jnp.concatenate(
                [z_ref[rows, QB0 + hh * SWA_DH:QB0 + (hh + 1) * SWA_DH] for hh in heads], axis=0)
            s = _dot_nt(qg.astype(BF16), keys) * (SWA_DH ** -0.5)
            if mask_initial:
                s = jnp.where(key_pos >= first_valid, s, -1e30)
            sink = jnp.zeros((SWA_GROUP * CHUNK, 1), F32)
            for g, hh in enumerate(heads):
                sink = jnp.where(row // CHUNK == g, sink_ref[hh], sink)
            m = jnp.maximum(jnp.max(s, axis=-1, keepdims=True), sink)
            p = jnp.exp(s - m)
            den = jnp.sum(p, axis=-1, keepdims=True) + jnp.exp(sink - m)
            o = _dot(p.astype(BF16), vals) / den
            for g, hh in enumerate(heads):
                c0 = RET_HEADS * RET_DV + hh * SWA_DH
                mix_ref[rows, c0:c0 + SWA_DH] = o[g * CHUNK:(g + 1) * CHUNK, :].astype(BF16)
        kwin_ref[0:WINDOW, :] = kwin_ref[CHUNK:, :]
        vwin_ref[0:WINDOW, :] = vwin_ref[CHUNK:, :]
        return carry

    lax.fori_loop(0, chunks, chunk_step, 0)

    o_ref[...] = x + _dot(mix_ref[...], wout_ref[...])
    knew_ref[...] = kwin_ref[0:WINDOW, :]
    vnew_ref[...] = vwin_ref[0:WINDOW, :]
    sfin_ref[...] = s_ref[...]


def _retention_tables():
    h = np.arange(RET_HEADS, dtype=np.float64)
    log_g = np.log1p(-np.exp2(-5.0 - h))
    idx = np.arange(CHUNK, dtype=np.float64)
    intra = np.exp(np.abs(idx[:, None] - idx[None, :])[None] * log_g[:, None, None])
    q_dec = np.exp((idx + 1.0)[:, None] * log_g[None, :])
    k_dec = np.exp((CHUNK - 1.0 - idx)[:, None] * log_g[None, :])
    chunk_dec = tuple(float(v) for v in np.exp(CHUNK * log_g))
    q_dec = np.repeat(q_dec, RET_DK, axis=1)
    k_dec = np.repeat(k_dec, RET_DK, axis=1)
    return (jnp.asarray(intra, F32), jnp.asarray(q_dec, F32), jnp.asarray(k_dec, F32), chunk_dec)


def _rope_tables(pos):
    half = SWA_DH // 2
    inv = jnp.exp(-math.log(ROPE_THETA) * jnp.arange(half, dtype=F32) / half)
    ang = pos.astype(F32)[:, None] * inv[None, :]
    cos, sin = jnp.cos(ang), jnp.sin(ang)
    cos = jnp.concatenate([cos, cos] * (LANES // SWA_DH), axis=-1)
    sin = jnp.concatenate([-sin, sin] * (LANES // SWA_DH), axis=-1)
    return cos, sin


def _even_mixer(x, pos, kprev, vprev, s0, mask_initial, p):
    b, l, d = x.shape
    tq = min(512, l)
    chunks = tq // CHUNK
    intra, q_dec, k_dec, chunk_decay = _retention_tables()
    cos, sin = _rope_tables(pos)
    nqk = (SWA_HEADS + SWA_KV_HEADS) * SWA_DH
    group = np.arange(nqk) // SWA_DH
    ones_blk = jnp.asarray(group[:, None] == group[None, :], BF16)

    def const(shape):
        return pl.BlockSpec(shape, lambda bi, ti: (0,) * len(shape))

    kv_spec = pl.BlockSpec((None, WINDOW, LANES), lambda bi, ti: (bi, 0, 0))
    st_spec = pl.BlockSpec((None, RET_HEADS, RET_DK, RET_DV), lambda bi, ti: (bi, 0, 0, 0))
    x_spec = pl.BlockSpec((None, tq, d), lambda bi, ti: (bi, ti, 0))
    tab_spec = pl.BlockSpec((tq, LANES), lambda bi, ti: (ti, 0))
    kern = functools.partial(_even_kernel, chunks=chunks, mask_initial=mask_initial, chunk_decay=chunk_decay)
    return pl.pallas_call(
        kern,
        grid=(b, l // tq),
        in_specs=[
            pl.BlockSpec(memory_space=pltpu.SMEM),
            x_spec, const((1, d)), const((d, IN0_COLS)), const((OUT0_COLS, d)),
            tab_spec, tab_spec,
            const((RET_HEADS, CHUNK, CHUNK)), const((CHUNK, RET_HEADS * RET_DK)), const((CHUNK, RET_HEADS * RET_DK)),
            const((1, RET_HEADS * RET_DV)), const((1, SWA_HEADS * SWA_DH)), const((1, SWA_KV_HEADS * SWA_DH)),
            const((nqk, nqk)), kv_spec, kv_spec, st_spec,
        ],
        out_specs=[x_spec, kv_spec, kv_spec, st_spec],
        out_shape=[
            jax.ShapeDtypeStruct((b, l, d), F32),
            jax.ShapeDtypeStruct((b, WINDOW, LANES), F32),
            jax.ShapeDtypeStruct((b, WINDOW, LANES), F32),
            jax.ShapeDtypeStruct((b, RET_HEADS, RET_DK, RET_DV), F32),
        ],
        scratch_shapes=[
            pltpu.VMEM((tq, IN0_COLS), F32),
            pltpu.VMEM((tq, OUT0_COLS), BF16),
            pltpu.VMEM((WINDOW + CHUNK, LANES), F32),
            pltpu.VMEM((WINDOW + CHUNK, LANES), F32),
            pltpu.VMEM((RET_HEADS, RET_DK, RET_DV), F32),
        ],
        compiler_params=pltpu.CompilerParams(
            dimension_semantics=("arbitrary", "arbitrary"), vmem_limit_bytes=VMEM_LIMIT),
        name="even_mixer",
    )(p["sinks"], x, p["norm0_mix"], p["w_in0"], p["w_out0"], cos, sin, intra, q_dec, k_dec,
      p["ret_gn"], p["qnorm"], p["knorm"], ones_blk, kprev, vprev, s0)


SSM_SLABS = SSM_COLS // LANES
SCAN_SLABS = 4
BATCH_ROWS = 8
SCAN_ROW_PAD = 8


def _odd_kernel(x_ref, g_ref, win_ref, bbd_ref, lre_ref, lim_ref, cbd_ref, d_ref, wglu_ref, cw_ref, cb_ref,
                lng_ref, lnb_ref, wout_ref, s0re_ref, s0im_ref, conv0_ref,
                o_ref, sre_ref, sim_ref, convn_ref,
                bu_ref, hre_ref, him_ref, xp_ref, cv_ref, *, tm):
    t = pl.program_id(1)
    rows = BATCH_ROWS * tm

    @pl.when(t == 0)
    def _():
        for j in range(SSM_SLABS):
            hre_ref[j] = s0re_ref[:, j * LANES:(j + 1) * LANES]
            him_ref[j] = s0im_ref[:, j * LANES:(j + 1) * LANES]
        xp_ref[:, 0:CONV_PAD, :] = conv0_ref[...]

    x = x_ref[...].reshape(rows, D_MODEL)
    xn = _rms_norm(x, g_ref[...]).astype(BF16)
    z = _dot(xn, win_ref[...])
    u = z[:, 0:SSM_WIDTH]
    d_val = z[:, SSM_WIDTH:SSM_WIDTH + CONV_WIDTH]
    d_gate = z[:, SSM_WIDTH + CONV_WIDTH:]

    pitch = tm + SCAN_ROW_PAD
    bu = _dot(u.astype(BF16), bbd_ref[...])
    for j in range(2 * SSM_SLABS):
        for b in range(BATCH_ROWS):
            bu_ref[j, b * pitch:b * pitch + tm, :] = bu[b * tm:(b + 1) * tm, j * LANES:(j + 1) * LANES]
    for j0 in range(0, SSM_SLABS, SCAN_SLABS):
        slabs = range(j0, j0 + SCAN_SLABS)
        lr = [lre_ref[j] for j in slabs]
        li = [lim_ref[j] for j in slabs]

        def step(ti, carry):
            at_t = pl.ds(ti, BATCH_ROWS, stride=pitch)
            out = []
            for n, j in enumerate(slabs):
                hr, hi = carry[n]
                nr = lr[n] * hr - li[n] * hi + bu_ref[j, at_t, :]
                ni = lr[n] * hi + li[n] * hr + bu_ref[SSM_SLABS + j, at_t, :]
                bu_ref[j, at_t, :] = nr
                bu_ref[SSM_SLABS + j, at_t, :] = ni
                out.append((nr, ni))
            return tuple(out)

        init = tuple((hre_ref[j], him_ref[j]) for j in slabs)
        final = lax.fori_loop(0, tm, step, init, unroll=4)
        for n, j in enumerate(slabs):
            hre_ref[j], him_ref[j] = final[n]

    def slab_states(j):
        return jnp.concatenate([bu_ref[j, b * pitch:b * pitch + tm, :] for b in range(BATCH_ROWS)], axis=0)

    states = jnp.concatenate([slab_states(j).astype(BF16) for j in range(2 * SSM_SLABS)], axis=-1)
    y = _dot(states, cbd_ref[...]) + d_ref[...] * u
    g = 0.5 * y * (1.0 + lax.erf(y * (2.0 ** -0.5)))
    o_c = g * jax.nn.sigmoid(_dot(g.astype(BF16), wglu_ref[...]))

    u_d = d_val * jax.nn.sigmoid(d_gate)
    xp_ref[:, CONV_PAD:, :] = u_d.reshape(BATCH_ROWS, tm, CONV_WIDTH)
    first = CONV_PAD - (CONV_K - 1)
    sub = 8

    def conv_sequence(b, carry):
        for c0 in range(0, CONV_WIDTH, LANES):
            cols = slice(c0, c0 + LANES)
            blk = jnp.zeros((tm, LANES), F32) + cb_ref[:, cols]
            for r in range(sub):
                taps = range(r, CONV_K, sub)
                shifted = xp_ref[b, first + r:first + r + tm + sub * (len(taps) - 1), cols]
                for q, k in enumerate(taps):
                    blk = blk + shifted[sub * q:sub * q + tm, :] * cw_ref[k:k + 1, cols]
            cv_ref[b, :, cols] = blk
        return carry

    lax.fori_loop(0, BATCH_ROWS, conv_sequence, 0)
    acc = cv_ref[...]
    mu = jnp.mean(acc, axis=-1, keepdims=True)
    xc = acc - mu
    var = jnp.mean(xc * xc, axis=-1, keepdims=True)
    o_d = _silu(xc * lax.rsqrt(var + EPS) * lng_ref[...] + lnb_ref[...]).reshape(rows, CONV_WIDTH)
    xp_ref[:, 0:CONV_PAD, :] = xp_ref[:, tm:tm + CONV_PAD, :]

    mix = jnp.concatenate([o_c.astype(BF16), o_d.astype(BF16)], axis=-1)
    o_ref[...] = (x + _dot(mix, wout_ref[...])).reshape(BATCH_ROWS, tm, D_MODEL)
    for j in range(SSM_SLABS):
        sre_ref[:, j * LANES:(j + 1) * LANES] = hre_ref[j]
        sim_ref[:, j * LANES:(j + 1) * LANES] = him_ref[j]
    convn_ref[...] = xp_ref[:, 0:CONV_PAD, :]


def _s5_discretize(p):
    dt = jnp.exp(p["ssm_log_dt"].astype(F32))[:, None]
    are, aim = p["ssm_a_re"].astype(F32), p["ssm_a_im"].astype(F32)
    mag = jnp.exp(are * dt)
    lb_re, lb_im = mag * jnp.cos(aim * dt), mag * jnp.sin(aim * dt)
    den = are * are + aim * aim
    nr, ni = lb_re - 1.0, lb_im
    f_re = (nr * are + ni * aim) / den
    f_im = (ni * are - nr * aim) / den
    br, bi = p["ssm_b_re"].astype(F32), p["ssm_b_im"].astype(F32)
    bb_re = f_re[..., None] * br - f_im[..., None] * bi
    bb_im = f_re[..., None] * bi + f_im[..., None] * br
    return lb_re, lb_im, bb_re, bb_im


def _s5_operands(p):
    lb_re, lb_im, bb_re, bb_im = _s5_discretize(p)
    eye = jnp.eye(SSM_GROUPS, dtype=F32)

    def in_map(bb):
        return jnp.einsum("gpc,gh->gchp", bb, eye).reshape(SSM_WIDTH, SSM_COLS)

    def out_map(c):
        return jnp.einsum("gcp,gh->gphc", c.astype(F32), eye).reshape(SSM_COLS, SSM_WIDTH)

    bbd = jnp.concatenate([in_map(bb_re), in_map(bb_im)], axis=1).astype(BF16)
    cbd = jnp.concatenate([out_map(p["ssm_c_re"]), -out_map(p["ssm_c_im"])], axis=0).astype(BF16)
    slab_shape = (SSM_SLABS, BATCH_ROWS, LANES)
    lre = jnp.broadcast_to(lb_re.reshape(SSM_SLABS, 1, LANES), slab_shape)
    lim = jnp.broadcast_to(lb_im.reshape(SSM_SLABS, 1, LANES), slab_shape)
    return bbd, cbd, lre, lim


def _odd_mixer(x, s0re, s0im, conv0, p):
    b, l, d = x.shape
    tm = min(64, l)
    bbd, cbd, lre, lim = _s5_operands(p)
    conv0 = jnp.pad(conv0, ((0, 0), (CONV_PAD - (CONV_K - 1), 0), (0, 0)))
    conv_w = jnp.pad(p["conv_w"], ((0, CONV_PAD - CONV_K), (0, 0)))

    def const(shape):
        return pl.BlockSpec(shape, lambda bi, ti: (0,) * len(shape))

    x_spec = pl.BlockSpec((BATCH_ROWS, tm, d), lambda bi, ti: (bi, ti, 0))
    st_spec = pl.BlockSpec((BATCH_ROWS, SSM_COLS), lambda bi, ti: (bi, 0))
    cv_spec = pl.BlockSpec((BATCH_ROWS, CONV_PAD, CONV_WIDTH), lambda bi, ti: (bi, 0, 0))
    out, sre, sim, convn = pl.pallas_call(
        functools.partial(_odd_kernel, tm=tm),
        grid=(b // BATCH_ROWS, l // tm),
        in_specs=[
            x_spec, const((1, d)), const((d, IN1_COLS)), const((SSM_WIDTH, 2 * SSM_COLS)),
            const((SSM_SLABS, BATCH_ROWS, LANES)), const((SSM_SLABS, BATCH_ROWS, LANES)),
            const((2 * SSM_COLS, SSM_WIDTH)),
            const((1, SSM_WIDTH)), const((SSM_WIDTH, SSM_WIDTH)), const((CONV_PAD, CONV_WIDTH)),
            const((1, CONV_WIDTH)), const((1, CONV_WIDTH)), const((1, CONV_WIDTH)),
            const((SSM_WIDTH + CONV_WIDTH, d)), st_spec, st_spec, cv_spec,
        ],
        out_specs=[x_spec, st_spec, st_spec, cv_spec],
        out_shape=[
            jax.ShapeDtypeStruct((b, l, d), F32),
            jax.ShapeDtypeStruct((b, SSM_COLS), F32),
            jax.ShapeDtypeStruct((b, SSM_COLS), F32),
            jax.ShapeDtypeStruct((b, CONV_PAD, CONV_WIDTH), F32),
        ],
        scratch_shapes=[
            pltpu.VMEM((2 * SSM_SLABS, BATCH_ROWS * (tm + SCAN_ROW_PAD), LANES), F32),
            pltpu.VMEM((SSM_SLABS, BATCH_ROWS, LANES), F32),
            pltpu.VMEM((SSM_SLABS, BATCH_ROWS, LANES), F32),
            pltpu.VMEM((BATCH_ROWS, CONV_PAD + tm, CONV_WIDTH), F32),
            pltpu.VMEM((BATCH_ROWS, tm, CONV_WIDTH), F32),
        ],
        compiler_params=pltpu.CompilerParams(
            dimension_semantics=("arbitrary", "arbitrary"), vmem_limit_bytes=VMEM_LIMIT),
        name="odd_mixer",
    )(x, p["norm1_mix"], p["w_in1"], bbd, lre, lim, cbd, p["ssm_d"], p["ssm_w_glu"], conv_w, p["conv_b"],
      p["conv_ln_g"], p["conv_ln_b"], p["w_out1"], s0re, s0im, conv0)
    shape3 = (b, SSM_GROUPS, SSM_STATE)
    return out, sre.reshape(shape3), sim.reshape(shape3), convn[:, CONV_PAD - (CONV_K - 1):, :]


def _ffn_kernel(x_ref, g_ref, wg_ref, wu_ref, wd_ref, o_ref, xn_ref):
    j = pl.program_id(1)

    @pl.when(j == 0)
    def _():
        x = x_ref[...]
        xn_ref[...] = _rms_norm(x, g_ref[...]).astype(BF16)
        o_ref[...] = x

    xn = xn_ref[...]
    hidden = _silu(_dot(xn, wg_ref[...])) * _dot(xn, wu_ref[...])
    o_ref[...] += _dot(hidden.astype(BF16), wd_ref[...])


def _dense_ffn(x, g, wg, wu, wd, *, tm, tf):
    n, d = x.shape
    f = wg.shape[1]
    x_spec = pl.BlockSpec((tm, d), lambda i, j: (i, 0))
    return pl.pallas_call(
        _ffn_kernel,
        grid=(n // tm, f // tf),
        in_specs=[
            x_spec, pl.BlockSpec((1, d), lambda i, j: (0, 0)),
            pl.BlockSpec((d, tf), lambda i, j: (0, j)), pl.BlockSpec((d, tf), lambda i, j: (0, j)),
            pl.BlockSpec((tf, d), lambda i, j: (j, 0)),
        ],
        out_specs=x_spec,
        out_shape=jax.ShapeDtypeStruct((n, d), F32),
        scratch_shapes=[pltpu.VMEM((tm, d), BF16)],
        compiler_params=pltpu.CompilerParams(
            dimension_semantics=("arbitrary", "arbitrary"), vmem_limit_bytes=VMEM_LIMIT),
        name="dense_ffn",
    )(x, g, wg, wu, wd)


ROUTE_GATE_LANE = 0
ROUTE_INDEX_LANE = 2


def _route_kernel(x_ref, g_ref, r_hi_ref, r_lo_ref, o_ref):
    xn = _rms_norm(x_ref[...], g_ref[...])
    logits = _split_dot(xn, r_hi_ref[...]) + _dot(xn.astype(BF16), r_lo_ref[...])
    lane = lax.broadcasted_iota(jnp.int32, logits.shape, 1)
    logits = jnp.where(lane < N_EXPERTS, logits, -jnp.inf)
    v1 = jnp.max(logits, axis=-1, keepdims=True)
    i1 = jnp.min(jnp.where(logits == v1, lane, LANES), axis=-1, keepdims=True)
    rest = jnp.where(lane == i1, -jnp.inf, logits)
    v2 = jnp.max(rest, axis=-1, keepdims=True)
    i2 = jnp.min(jnp.where(rest == v2, lane, LANES), axis=-1, keepdims=True)
    e2 = jnp.exp(v2 - v1)
    den = 1.0 + e2
    rec = jnp.where(lane == ROUTE_GATE_LANE, 1.0 / den, 0.0)
    rec = jnp.where(lane == ROUTE_GATE_LANE + 1, e2 / den, rec)
    rec = jnp.where(lane == ROUTE_INDEX_LANE, i1.astype(F32), rec)
    rec = jnp.where(lane == ROUTE_INDEX_LANE + 1, i2.astype(F32), rec)
    o_ref[...] = rec


def _gather_rows(src_hbm, dst_ref, index_ref, sem, count):
    def issue(r, carry):
        pltpu.make_async_copy(src_hbm.at[pl.ds(index_ref[r], 1), :], dst_ref.at[pl.ds(r, 1), :], sem).start()
        return carry

    lax.fori_loop(0, count, issue, 0, unroll=8)
    pltpu.make_async_copy(src_hbm.at[pl.ds(0, count), :], dst_ref, sem).wait()


def _expert_kernel(texp_ref, used_ref, src_ref, x_hbm, g_ref, wg_ref, wu_ref, wd_ref, y_ref, xg_ref, xn_ref, sem,
                   *, tm):
    i = pl.program_id(0)
    j = pl.program_id(1)
    used = used_ref[i] > 0

    @pl.when(jnp.logical_and(j == 0, used))
    def _():
        _gather_rows(x_hbm, xg_ref, src_ref, sem, tm)
        xn_ref[...] = _rms_norm(xg_ref[...], g_ref[...]).astype(BF16)

    @pl.when(jnp.logical_and(j == 0, jnp.logical_not(used)))
    def _():
        y_ref[...] = jnp.zeros_like(y_ref)

    @pl.when(used)
    def _():
        xn = xn_ref[...]
        hidden = _silu(_dot(xn, wg_ref[...])) * _dot(xn, wu_ref[...])
        part = _dot(hidden.astype(BF16), wd_ref[...])

        @pl.when(j == 0)
        def _():
            y_ref[...] = part

        @pl.when(j > 0)
        def _():
            y_ref[...] += part


def _combine_kernel(pos0_ref, pos1_ref, route_ref, x_ref, y_hbm, o_ref, y0_ref, y1_ref, sem, *, tc):
    _gather_rows(y_hbm, y0_ref, pos0_ref, sem.at[0], tc)
    _gather_rows(y_hbm, y1_ref, pos1_ref, sem.at[1], tc)
    route = route_ref[...]
    g0 = route[:, ROUTE_GATE_LANE:ROUTE_GATE_LANE + 1]
    g1 = route[:, ROUTE_GATE_LANE + 1:ROUTE_GATE_LANE + 2]
    o_ref[...] = x_ref[...] + (g0 * y0_ref[...] + g1 * y1_ref[...])


def _dispatch_plan(experts, tm, n_tiles):
    n = experts.shape[0]
    flat = experts.reshape(-1)
    onehot = (flat[:, None] == jnp.arange(N_EXPERTS, dtype=jnp.int32)[None, :]).astype(jnp.int32)
    csum = jnp.cumsum(onehot, axis=0)
    rank = jnp.sum((csum - onehot) * onehot, axis=1)
    counts = csum[-1]
    tiles = (counts + tm - 1) // tm
    tile_end = jnp.cumsum(tiles)
    tile_start = tile_end - tiles
    pos = tile_start[flat] * tm + rank
    src = jnp.zeros((n_tiles * tm,), jnp.int32).at[pos].set(jnp.arange(2 * n, dtype=jnp.int32) // 2)
    tile_ids = jnp.arange(n_tiles, dtype=jnp.int32)
    used = (tile_ids < tile_end[-1]).astype(jnp.int32)
    tile_expert = jnp.minimum(jnp.searchsorted(tile_end, tile_ids, side="right"), N_EXPERTS - 1).astype(jnp.int32)
    last_expert = tile_expert[jnp.maximum(tile_end[-1] - 1, 0)]
    tile_expert = jnp.where(used > 0, tile_expert, last_expert)
    pos = pos.reshape(n, 2)
    return src, pos[:, 0], pos[:, 1], tile_expert, used


def _moe_ffn(x, g, r_hi, r_lo, wg, wu, wd, *, tr, tm, tf, tc):
    n, d = x.shape
    e, _, f = wg.shape
    tpe = f // tf
    n_tiles = (TOP_K * n) // tm + e
    smem = pltpu.SMEM

    route = pl.pallas_call(
        _route_kernel,
        grid=(n // tr,),
        in_specs=[
            pl.BlockSpec((tr, d), lambda i: (i, 0)), pl.BlockSpec((1, d), lambda i: (0, 0)),
            pl.BlockSpec((d, LANES), lambda i: (0, 0)), pl.BlockSpec((d, LANES), lambda i: (0, 0)),
        ],
        out_specs=pl.BlockSpec((tr, LANES), lambda i: (i, 0)),
        out_shape=jax.ShapeDtypeStruct((n, LANES), F32),
        compiler_params=pltpu.CompilerParams(dimension_semantics=("arbitrary",), vmem_limit_bytes=VMEM_LIMIT),
        name="moe_route",
    )(x, g, r_hi, r_lo)

    experts = route[:, ROUTE_INDEX_LANE:ROUTE_INDEX_LANE + TOP_K].astype(jnp.int32)
    src, pos0, pos1, tile_expert, used = _dispatch_plan(experts, tm, n_tiles)

    def ff_tile(i, j, texp, used):
        return jnp.where(used[i] > 0, j, tpe - 1)

    y = pl.pallas_call(
        functools.partial(_expert_kernel, tm=tm),
        grid_spec=pltpu.PrefetchScalarGridSpec(
            num_scalar_prefetch=2,
            grid=(n_tiles, tpe),
            in_specs=[
                pl.BlockSpec((tm,), lambda i, j, texp, used: (i,), memory_space=smem),
                pl.BlockSpec(memory_space=pl.ANY),
                pl.BlockSpec((1, d), lambda i, j, texp, used: (0, 0)),
                pl.BlockSpec((None, d, tf), lambda i, j, texp, used: (texp[i], 0, ff_tile(i, j, texp, used))),
                pl.BlockSpec((None, d, tf), lambda i, j, texp, used: (texp[i], 0, ff_tile(i, j, texp, used))),
                pl.BlockSpec((None, tf, d), lambda i, j, texp, used: (texp[i], ff_tile(i, j, texp, used), 0)),
            ],
            out_specs=pl.BlockSpec((tm, d), lambda i, j, texp, used: (i, 0)),
            scratch_shapes=[pltpu.VMEM((tm, d), F32), pltpu.VMEM((tm, d), BF16), pltpu.SemaphoreType.DMA(())],
        ),
        out_shape=jax.ShapeDtypeStruct((n_tiles * tm, d), F32),
        compiler_params=pltpu.CompilerParams(
            dimension_semantics=("arbitrary", "arbitrary"), vmem_limit_bytes=VMEM_LIMIT),
        name="moe_experts",
    )(tile_expert, used, src, x, g, wg, wu, wd)

    return pl.pallas_call(
        functools.partial(_combine_kernel, tc=tc),
        grid=(n // tc,),
        in_specs=[
            pl.BlockSpec((tc,), lambda i: (i,), memory_space=smem),
            pl.BlockSpec((tc,), lambda i: (i,), memory_space=smem),
            pl.BlockSpec((tc, LANES), lambda i: (i, 0)),
            pl.BlockSpec((tc, d), lambda i: (i, 0)),
            pl.BlockSpec(memory_space=pl.ANY),
        ],
        out_specs=pl.BlockSpec((tc, d), lambda i: (i, 0)),
        out_shape=jax.ShapeDtypeStruct((n, d), F32),
        scratch_shapes=[pltpu.VMEM((tc, d), F32), pltpu.VMEM((tc, d), F32), pltpu.SemaphoreType.DMA((2,))],
        compiler_params=pltpu.CompilerParams(dimension_semantics=("arbitrary",), vmem_limit_bytes=VMEM_LIMIT),
        name="moe_combine",
    )(pos0, pos1, route, x, y)


def kernel(x_prompt, x_sample, cache_k_swa, cache_v_swa, state_ret, state_ssm_re, state_ssm_im, state_conv,
           norm0_mix, w_in0, ret_gn, swa_qnorm, swa_knorm, swa_sinks, w_out0, norm0_ffn, ffn0_w_gate,
           ffn0_w_up, ffn0_w_down, norm1_mix, w_in1, ssm_a_re, ssm_a_im, ssm_log_dt, ssm_b_re, ssm_b_im,
           ssm_c_re, ssm_c_im, ssm_d, ssm_w_glu, conv_w, conv_b, conv_ln_g, conv_ln_b, w_out1, norm1_ffn,
           moe_router, moe_w_gate, moe_w_up, moe_w_down):
    bp, lp, d = x_prompt.shape
    bs, ls, _ = x_sample.shape
    row = lambda v: v.astype(F32).reshape(1, -1)
    r_pad = jnp.pad(moe_router.astype(F32), ((0, 0), (0, LANES - N_EXPERTS)))
    r_hi = r_pad.astype(BF16)
    p = dict(
        sinks=swa_sinks.astype(F32), norm0_mix=row(norm0_mix), w_in0=w_in0.astype(BF16), w_out0=w_out0.astype(BF16),
        ret_gn=row(ret_gn), qnorm=row(jnp.tile(swa_qnorm, SWA_HEADS)), knorm=row(jnp.tile(swa_knorm, SWA_KV_HEADS)),
        norm1_mix=row(norm1_mix), w_in1=w_in1.astype(BF16), ssm_a_re=ssm_a_re, ssm_a_im=ssm_a_im,
        ssm_log_dt=ssm_log_dt, ssm_b_re=ssm_b_re, ssm_b_im=ssm_b_im, ssm_c_re=ssm_c_re, ssm_c_im=ssm_c_im,
        ssm_d=row(ssm_d), ssm_w_glu=ssm_w_glu.astype(BF16), conv_w=conv_w.astype(F32), conv_b=row(conv_b),
        conv_ln_g=row(conv_ln_g), conv_ln_b=row(conv_ln_b), w_out1=w_out1.astype(BF16),
    )
    ffn0 = (row(norm0_ffn), ffn0_w_gate.astype(BF16), ffn0_w_up.astype(BF16), ffn0_w_down.astype(BF16))
    moe = (row(norm1_ffn), r_hi, (r_pad - r_hi.astype(F32)).astype(BF16),
           moe_w_gate.astype(BF16), moe_w_up.astype(BF16), moe_w_down.astype(BF16))

    pos_p = jnp.arange(lp, dtype=jnp.int32)
    pos_s = PAST_LEN + jnp.arange(ls, dtype=jnp.int32)
    zeros_kv = jnp.zeros((bp, WINDOW, LANES), F32)
    zeros_ret = jnp.zeros((bp, RET_HEADS, RET_DK, RET_DV), F32)
    zeros_ssm = jnp.zeros((bp, SSM_COLS), F32)
    zeros_conv = jnp.zeros((bp, CONV_K - 1, CONV_WIDTH), F32)
    win = cache_k_swa.shape[1]

    h_p, k_p, v_p, ret_p = _even_mixer(x_prompt, pos_p, zeros_kv, zeros_kv, zeros_ret, True, p)
    h_s, k_s, v_s, ret_s = _even_mixer(
        x_sample, pos_s, cache_k_swa.reshape(bs, win, LANES), cache_v_swa.reshape(bs, win, LANES), state_ret,
        False, p)

    h_p = _dense_ffn(h_p.reshape(bp * lp, d), *ffn0, tm=1024, tf=1408).reshape(bp, lp, d)
    h_s = _dense_ffn(h_s.reshape(bs * ls, d), *ffn0, tm=1024, tf=1408).reshape(bs, ls, d)

    h_p, sre_p, sim_p, conv_p = _odd_mixer(h_p, zeros_ssm, zeros_ssm, zeros_conv, p)
    h_s, sre_s, sim_s, conv_s = _odd_mixer(
        h_s, state_ssm_re.reshape(bs, SSM_COLS), state_ssm_im.reshape(bs, SSM_COLS), state_conv, p)

    h_p = _moe_ffn(h_p.reshape(bp * lp, d), *moe, tr=1024, tm=1024, tf=512, tc=512).reshape(bp, lp, d)
    h_s = _moe_ffn(h_s.reshape(bs * ls, d), *moe, tr=1024, tm=256, tf=512, tc=512).reshape(bs, ls, d)

    kv_p = (bp, WINDOW, SWA_KV_HEADS, SWA_DH)
    kv_s = (bs, win, SWA_KV_HEADS, SWA_DH)
    return (h_p, h_s, k_p.reshape(kv_p), v_p.reshape(kv_p), k_s.reshape(kv_s), v_s.reshape(kv_s), ret_p, ret_s,
            sre_p, sim_p, sre_s, sim_s, conv_p, conv_s)
```

```python
import functools
import math

import jax
import jax.numpy as jnp
import numpy as np
from jax import lax
from jax.experimental import pallas as pl
from jax.experimental.pallas import tpu as pltpu

F32 = jnp.float32
BF16 = jnp.bfloat16

D_MODEL = 1024
CHUNK = 64
EPS = 1e-6
ROPE_THETA = 10000.0
PAST_LEN = 4096
RET_HEADS, RET_DK, RET_DV = 4, 64, 128
SWA_HEADS, SWA_KV_HEADS, SWA_DH = 8, 2, 64
SWA_GROUP = SWA_HEADS // SWA_KV_HEADS
WINDOW = 128
SSM_WIDTH = D_MODEL // 2
SSM_GROUP = 16
SSM_GROUPS = SSM_WIDTH // SSM_GROUP
SSM_STATE = 64
SSM_COLS = SSM_GROUPS * SSM_STATE
CONV_WIDTH = D_MODEL // 2
CONV_K = 31
CONV_PAD = 32
N_EXPERTS = 8
TOP_K = 2

QA0, KA0, VA0, GA0 = 0, 256, 512, 1024
QB0, KB0, VB0 = 1536, 2048, 2176
IN0_COLS = 2304
OUT0_COLS = 1024
IN1_COLS = 1536

LANES = 128
VMEM_LIMIT = 56 * 1024 * 1024


def _rms_norm(x, g):
    return x * lax.rsqrt(jnp.mean(x * x, axis=-1, keepdims=True) + EPS) * g


def _silu(x):
    return x * jax.nn.sigmoid(x)


def _dot(a, b):
    return jnp.dot(a, b, preferred_element_type=F32)


def _dot_nt(a, b):
    return lax.dot_general(a, b, (((1,), (1,)), ((), ())), preferred_element_type=F32)


def _dot_tn(a, b):
    return lax.dot_general(a, b, (((0,), (0,)), ((), ())), preferred_element_type=F32)


def _rot_half(x):
    w = x.shape[-1]
    lane = lax.broadcasted_iota(jnp.int32, x.shape, x.ndim - 1)
    fwd = pltpu.roll(x, w - SWA_DH // 2, x.ndim - 1)
    bwd = pltpu.roll(x, SWA_DH // 2, x.ndim - 1)
    return jnp.where(lane % SWA_DH < SWA_DH // 2, fwd, bwd)


def _rope(x, cos, sin_signed):
    reps = x.shape[-1] // LANES
    if reps > 1:
        cos = jnp.concatenate([cos] * reps, axis=-1)
        sin_signed = jnp.concatenate([sin_signed] * reps, axis=-1)
    return x * cos + _rot_half(x) * sin_signed


def _split_dot(x, w_bf16):
    hi = x.astype(BF16)
    lo = (x - hi.astype(F32)).astype(BF16)
    return _dot(hi, w_bf16) + _dot(lo, w_bf16)


def _even_kernel(sink_ref, x_ref, g_ref, win_ref, wout_ref, cos_ref, sin_ref, intra_ref, qdec_ref, kdec_ref,
                 gn_ref, qn_ref, kn_ref, ones_ref, kprev_ref, vprev_ref, s0_ref,
                 o_ref, knew_ref, vnew_ref, sfin_ref,
                 z_ref, mix_ref, kwin_ref, vwin_ref, s_ref, *, chunks, mask_initial, chunk_decay):
    t = pl.program_id(1)

    @pl.when(t == 0)
    def _():
        kwin_ref[0:WINDOW, :] = kprev_ref[...]
        vwin_ref[0:WINDOW, :] = vprev_ref[...]
        s_ref[...] = s0_ref[...]

    x = x_ref[...]
    xn = _rms_norm(x, g_ref[...]).astype(BF16)
    z_ref[...] = _dot(xn, win_ref[...])

    cos = cos_ref[...]
    sin = sin_ref[...]
    z_ref[:, QA0:KA0] = _rope(z_ref[:, QA0:KA0], cos, sin)
    z_ref[:, KA0:VA0] = _rope(z_ref[:, KA0:VA0], cos, sin) * (RET_DK ** -0.5)
    qk = z_ref[:, QB0:VB0]
    ms = _split_dot(qk * qk, ones_ref[...]) * (1.0 / SWA_DH)
    qk = qk * lax.rsqrt(ms + EPS)
    nq = SWA_HEADS * SWA_DH
    z_ref[:, QB0:KB0] = _rope(qk[:, :nq] * qn_ref[...], cos, sin)
    z_ref[:, KB0:VB0] = _rope(qk[:, nq:] * kn_ref[...], cos, sin)

    row = lax.broadcasted_iota(jnp.int32, (SWA_GROUP * CHUNK, 1), 0)
    key_pos = lax.broadcasted_iota(jnp.int32, (1, WINDOW + CHUNK), 1)

    def chunk_step(c, carry):
        r0 = pl.multiple_of(c * CHUNK, CHUNK)
        rows = pl.ds(r0, CHUNK)
        for h in range(RET_HEADS):
            q = z_ref[rows, QA0 + h * RET_DK:QA0 + (h + 1) * RET_DK]
            k = z_ref[rows, KA0 + h * RET_DK:KA0 + (h + 1) * RET_DK]
            v = z_ref[rows, VA0 + h * RET_DV:VA0 + (h + 1) * RET_DV].astype(BF16)
            gate = z_ref[rows, GA0 + h * RET_DV:GA0 + (h + 1) * RET_DV]
            qd = q * qdec_ref[:, h * RET_DK:(h + 1) * RET_DK]
            kd = k * kdec_ref[:, h * RET_DK:(h + 1) * RET_DK]
            scores = _dot_nt(q.astype(BF16), k.astype(BF16)) * intra_ref[h]
            state = s_ref[h]
            o = _dot(scores.astype(BF16), v) + _dot(qd.astype(BF16), state.astype(BF16))
            s_ref[h] = chunk_decay[h] * state + _dot_tn(kd.astype(BF16), v)
            oc = o - jnp.mean(o, axis=-1, keepdims=True)
            on = oc * lax.rsqrt(jnp.mean(oc * oc, axis=-1, keepdims=True) + EPS)
            on = on * gn_ref[:, h * RET_DV:(h + 1) * RET_DV]
            mix_ref[rows, h * RET_DV:(h + 1) * RET_DV] = (_silu(gate) * on).astype(BF16)
        kwin_ref[WINDOW:, :] = z_ref[rows, KB0:VB0]
        vwin_ref[WINDOW:, :] = z_ref[rows, VB0:IN0_COLS]
        if mask_initial:
            first_valid = WINDOW - (t * chunks + c) * CHUNK
        for j in range(SWA_KV_HEADS):
            keys = kwin_ref[:, j * SWA_DH:(j + 1) * SWA_DH].astype(BF16)
            vals = vwin_ref[:, j * SWA_DH:(j + 1) * SWA_DH].astype(BF16)
            heads = [SWA_GROUP * j + g for g in range(SWA_GROUP)]
            qg = jnp.concatenate(
                [z_ref[rows, QB0 + hh * SWA_DH:QB0 + (hh + 1) * SWA_DH] for hh in heads], axis=0)
            s = _dot_nt(qg.astype(BF16), keys) * (SWA_DH ** -0.5)
            if mask_initial:
                s = jnp.where(key_pos >= first_valid, s, -1e30)
            sink = jnp.zeros((SWA_GROUP * CHUNK, 1), F32)
            for g, hh in enumerate(heads):
                sink = jnp.where(row // CHUNK == g, sink_ref[hh], sink)
            m = jnp.maximum(jnp.max(s, axis=-1, keepdims=True), sink)
            p = jnp.exp(s - m)
            den = jnp.sum(p, axis=-1, keepdims=True) + jnp.exp(sink - m)
            o = _dot(p.astype(BF16), vals) / den
            for g, hh in enumerate(heads):
                c0 = RET_HEADS * RET_DV + hh * SWA_DH
                mix_ref[rows, c0:c0 + SWA_DH] = o[g * CHUNK:(g + 1) * CHUNK, :].astype(BF16)
        kwin_ref[0:WINDOW, :] = kwin_ref[CHUNK:, :]
        vwin_ref[0:WINDOW, :] = vwin_ref[CHUNK:, :]
        return carry

    lax.fori_loop(0, chunks, chunk_step, 0, unroll=min(4, chunks))

    o_ref[...] = x + _dot(mix_ref[...], wout_ref[...])
    knew_ref[...] = kwin_ref[0:WINDOW, :]
    vnew_ref[...] = vwin_ref[0:WINDOW, :]
    sfin_ref[...] = s_ref[...]


def _retention_tables():
    h = np.arange(RET_HEADS, dtype=np.float64)
    log_g = np.log1p(-np.exp2(-5.0 - h))
    idx = np.arange(CHUNK, dtype=np.float64)
    intra = np.exp(np.abs(idx[:, None] - idx[None, :])[None] * log_g[:, None, None])
    q_dec = np.exp((idx + 1.0)[:, None] * log_g[None, :])
    k_dec = np.exp((CHUNK - 1.0 - idx)[:, None] * log_g[None, :])
    chunk_dec = tuple(float(v) for v in np.exp(CHUNK * log_g))
    q_dec = np.repeat(q_dec, RET_DK, axis=1)
    k_dec = np.repeat(k_dec, RET_DK, axis=1)
    return (jnp.asarray(intra, F32), jnp.asarray(q_dec, F32), jnp.asarray(k_dec, F32), chunk_dec)


def _rope_tables(pos):
    half = SWA_DH // 2
    inv = jnp.exp(-math.log(ROPE_THETA) * jnp.arange(half, dtype=F32) / half)
    ang = pos.astype(F32)[:, None] * inv[None, :]
    cos, sin = jnp.cos(ang), jnp.sin(ang)
    cos = jnp.concatenate([cos, cos] * (LANES // SWA_DH), axis=-1)
    sin = jnp.concatenate([-sin, sin] * (LANES // SWA_DH), axis=-1)
    return cos, sin


def _even_mixer(x, pos, kprev, vprev, s0, mask_initial, p):
    b, l, d = x.shape
    tq = min(512, l)
    chunks = tq // CHUNK
    intra, q_dec, k_dec, chunk_decay = _retention_tables()
    cos, sin = _rope_tables(pos)
    nqk = (SWA_HEADS + SWA_KV_HEADS) * SWA_DH
    group = np.arange(nqk) // SWA_DH
    ones_blk = jnp.asarray(group[:, None] == group[None, :], BF16)

    def const(shape):
        return pl.BlockSpec(shape, lambda bi, ti: (0,) * len(shape))

    kv_spec = pl.BlockSpec((None, WINDOW, LANES), lambda bi, ti: (bi, 0, 0))
    st_spec = pl.BlockSpec((None, RET_HEADS, RET_DK, RET_DV), lambda bi, ti: (bi, 0, 0, 0))
    x_spec = pl.BlockSpec((None, tq, d), lambda bi, ti: (bi, ti, 0))
    tab_spec = pl.BlockSpec((tq, LANES), lambda bi, ti: (ti, 0))
    kern = functools.partial(_even_kernel, chunks=chunks, mask_initial=mask_initial, chunk_decay=chunk_decay)
    return pl.pallas_call(
        kern,
        grid=(b, l // tq),
        in_specs=[
            pl.BlockSpec(memory_space=pltpu.SMEM),
            x_spec, const((1, d)), const((d, IN0_COLS)), const((OUT0_COLS, d)),
            tab_spec, tab_spec,
            const((RET_HEADS, CHUNK, CHUNK)), const((CHUNK, RET_HEADS * RET_DK)), const((CHUNK, RET_HEADS * RET_DK)),
            const((1, RET_HEADS * RET_DV)), const((1, SWA_HEADS * SWA_DH)), const((1, SWA_KV_HEADS * SWA_DH)),
            const((nqk, nqk)), kv_spec, kv_spec, st_spec,
        ],
        out_specs=[x_spec, kv_spec, kv_spec, st_spec],
        out_shape=[
            jax.ShapeDtypeStruct((b, l, d), F32),
            jax.ShapeDtypeStruct((b, WINDOW, LANES), F32),
            jax.ShapeDtypeStruct((b, WINDOW, LANES), F32),
            jax.ShapeDtypeStruct((b, RET_HEADS, RET_DK, RET_DV), F32),
        ],
        scratch_shapes=[
            pltpu.VMEM((tq, IN0_COLS), F32),
            pltpu.VMEM((tq, OUT0_COLS), BF16),
            pltpu.VMEM((WINDOW + CHUNK, LANES), F32),
            pltpu.VMEM((WINDOW + CHUNK, LANES), F32),
            pltpu.VMEM((RET_HEADS, RET_DK, RET_DV), F32),
        ],
        compiler_params=pltpu.CompilerParams(
            dimension_semantics=("arbitrary", "arbitrary"), vmem_limit_bytes=VMEM_LIMIT),
        name="even_mixer",
    )(p["sinks"], x, p["norm0_mix"], p["w_in0"], p["w_out0"], cos, sin, intra, q_dec, k_dec,
      p["ret_gn"], p["qnorm"], p["knorm"], ones_blk, kprev, vprev, s0)


SSM_SLABS = SSM_COLS // LANES
SCAN_SLABS = 4
BATCH_ROWS = 8
SCAN_ROW_PAD = 8


def _odd_kernel(x_ref, g_ref, win_ref, bbd_ref, lre_ref, lim_ref, cbd_ref, d_ref, wglu_ref, cw_ref, cb_ref,
                lng_ref, lnb_ref, wout_ref, s0re_ref, s0im_ref, conv0_ref,
                o_ref, sre_ref, sim_ref, convn_ref,
                bu_ref, hre_ref, him_ref, xp_ref, cv_ref, *, tm):
    t = pl.program_id(1)
    rows = BATCH_ROWS * tm

    @pl.when(t == 0)
    def _():
        for j in range(SSM_SLABS):
            hre_ref[j] = s0re_ref[:, j * LANES:(j + 1) * LANES]
            him_ref[j] = s0im_ref[:, j * LANES:(j + 1) * LANES]
        xp_ref[:, 0:CONV_PAD, :] = conv0_ref[...]

    x = x_ref[...].reshape(rows, D_MODEL)
    xn = _rms_norm(x, g_ref[...]).astype(BF16)
    z = _dot(xn, win_ref[...])
    u = z[:, 0:SSM_WIDTH]
    d_val = z[:, SSM_WIDTH:SSM_WIDTH + CONV_WIDTH]
    d_gate = z[:, SSM_WIDTH + CONV_WIDTH:]

    pitch = tm + SCAN_ROW_PAD
    bu = _dot(u.astype(BF16), bbd_ref[...])
    for j in range(2 * SSM_SLABS):
        for b in range(BATCH_ROWS):
            bu_ref[j, b * pitch:b * pitch + tm, :] = bu[b * tm:(b + 1) * tm, j * LANES:(j + 1) * LANES]
    for j0 in range(0, SSM_SLABS, SCAN_SLABS):
        slabs = range(j0, j0 + SCAN_SLABS)
        lr = [lre_ref[j] for j in slabs]
        li = [lim_ref[j] for j in slabs]

        def step(ti, carry):
            at_t = pl.ds(ti, BATCH_ROWS, stride=pitch)
            out = []
            for n, j in enumerate(slabs):
                hr, hi = carry[n]
                nr = lr[n] * hr - li[n] * hi + bu_ref[j, at_t, :]
                ni = lr[n] * hi + li[n] * hr + bu_ref[SSM_SLABS + j, at_t, :]
                bu_ref[j, at_t, :] = nr
                bu_ref[SSM_SLABS + j, at_t, :] = ni
                out.append((nr, ni))
            return tuple(out)

        init = tuple((hre_ref[j], him_ref[j]) for j in slabs)
        final = lax.fori_loop(0, tm, step, init, unroll=4)
        for n, j in enumerate(slabs):
            hre_ref[j], him_ref[j] = final[n]

    def slab_states(j):
        return jnp.concatenate([bu_ref[j, b * pitch:b * pitch + tm, :] for b in range(BATCH_ROWS)], axis=0)

    states = jnp.concatenate([slab_states(j).astype(BF16) for j in range(2 * SSM_SLABS)], axis=-1)
    y = _dot(states, cbd_ref[...]) + d_ref[...] * u
    g = 0.5 * y * (1.0 + lax.erf(y * (2.0 ** -0.5)))
    o_c = g * jax.nn.sigmoid(_dot(g.astype(BF16), wglu_ref[...]))

    u_d = d_val * jax.nn.sigmoid(d_gate)
    xp_ref[:, CONV_PAD:, :] = u_d.reshape(BATCH_ROWS, tm, CONV_WIDTH)
    first = CONV_PAD - (CONV_K - 1)
    sub = 8

    def conv_sequence(b, carry):
        for c0 in range(0, CONV_WIDTH, LANES):
            cols = slice(c0, c0 + LANES)
            blk = jnp.zeros((tm, LANES), F32) + cb_ref[:, cols]
            for r in range(sub):
                taps = range(r, CONV_K, sub)
                shifted = xp_ref[b, first + r:first + r + tm + sub * (len(taps) - 1), cols]
                for q, k in enumerate(taps):
                    blk = blk + shifted[sub * q:sub * q + tm, :] * cw_ref[k:k + 1, cols]
            cv_ref[b, :, cols] = blk
        return carry

    lax.fori_loop(0, BATCH_ROWS, conv_sequence, 0)
    acc = cv_ref[...]
    mu = jnp.mean(acc, axis=-1, keepdims=True)
    xc = acc - mu
    var = jnp.mean(xc * xc, axis=-1, keepdims=True)
    o_d = _silu(xc * lax.rsqrt(var + EPS) * lng_ref[...] + lnb_ref[...]).reshape(rows, CONV_WIDTH)
    xp_ref[:, 0:CONV_PAD, :] = xp_ref[:, tm:tm + CONV_PAD, :]

    mix = jnp.concatenate([o_c.astype(BF16), o_d.astype(BF16)], axis=-1)
    o_ref[...] = (x + _dot(mix, wout_ref[...])).reshape(BATCH_ROWS, tm, D_MODEL)
    for j in range(SSM_SLABS):
        sre_ref[:, j * LANES:(j + 1) * LANES] = hre_ref[j]
        sim_ref[:, j * LANES:(j + 1) * LANES] = him_ref[j]
    convn_ref[...] = xp_ref[:, 0:CONV_PAD, :]


def _s5_discretize(p):
    dt = jnp.exp(p["ssm_log_dt"].astype(F32))[:, None]
    are, aim = p["ssm_a_re"].astype(F32), p["ssm_a_im"].astype(F32)
    mag = jnp.exp(are * dt)
    lb_re, lb_im = mag * jnp.cos(aim * dt), mag * jnp.sin(aim * dt)
    den = are * are + aim * aim
    nr, ni = lb_re - 1.0, lb_im
    f_re = (nr * are + ni * aim) / den
    f_im = (ni * are - nr * aim) / den
    br, bi = p["ssm_b_re"].astype(F32), p["ssm_b_im"].astype(F32)
    bb_re = f_re[..., None] * br - f_im[..., None] * bi
    bb_im = f_re[..., None] * bi + f_im[..., None] * br
    return lb_re, lb_im, bb_re, bb_im


def _s5_operands(p):
    lb_re, lb_im, bb_re, bb_im = _s5_discretize(p)
    eye = jnp.eye(SSM_GROUPS, dtype=F32)

    def in_map(bb):
        return jnp.einsum("gpc,gh->gchp", bb, eye).reshape(SSM_WIDTH, SSM_COLS)

    def out_map(c):
        return jnp.einsum("gcp,gh->gphc", c.astype(F32), eye).reshape(SSM_COLS, SSM_WIDTH)

    bbd = jnp.concatenate([in_map(bb_re), in_map(bb_im)], axis=1).astype(BF16)
    cbd = jnp.concatenate([out_map(p["ssm_c_re"]), -out_map(p["ssm_c_im"])], axis=0).astype(BF16)
    slab_shape = (SSM_SLABS, BATCH_ROWS, LANES)
    lre = jnp.broadcast_to(lb_re.reshape(SSM_SLABS, 1, LANES), slab_shape)
    lim = jnp.broadcast_to(lb_im.reshape(SSM_SLABS, 1, LANES), slab_shape)
    return bbd, cbd, lre, lim


def _odd_mixer(x, s0re, s0im, conv0, p):
    b, l, d = x.shape
    tm = min(64, l)
    bbd, cbd, lre, lim = _s5_operands(p)
    conv0 = jnp.pad(conv0, ((0, 0), (CONV_PAD - (CONV_K - 1), 0), (0, 0)))
    conv_w = jnp.pad(p["conv_w"], ((0, CONV_PAD - CONV_K), (0, 0)))

    def const(shape):
        return pl.BlockSpec(shape, lambda bi, ti: (0,) * len(shape))

    x_spec = pl.BlockSpec((BATCH_ROWS, tm, d), lambda bi, ti: (bi, ti, 0))
    st_spec = pl.BlockSpec((BATCH_ROWS, SSM_COLS), lambda bi, ti: (bi, 0))
    cv_spec = pl.BlockSpec((BATCH_ROWS, CONV_PAD, CONV_WIDTH), lambda bi, ti: (bi, 0, 0))
    out, sre, sim, convn = pl.pallas_call(
        functools.partial(_odd_kernel, tm=tm),
        grid=(b // BATCH_ROWS, l // tm),
        in_specs=[
            x_spec, const((1, d)), const((d, IN1_COLS)), const((SSM_WIDTH, 2 * SSM_COLS)),
            const((SSM_SLABS, BATCH_ROWS, LANES)), const((SSM_SLABS, BATCH_ROWS, LANES)),
            const((2 * SSM_COLS, SSM_WIDTH)),
            const((1, SSM_WIDTH)), const((SSM_WIDTH, SSM_WIDTH)), const((CONV_PAD, CONV_WIDTH)),
            const((1, CONV_WIDTH)), const((1, CONV_WIDTH)), const((1, CONV_WIDTH)),
            const((SSM_WIDTH + CONV_WIDTH, d)), st_spec, st_spec, cv_spec,
        ],
        out_specs=[x_spec, st_spec, st_spec, cv_spec],
        out_shape=[
            jax.ShapeDtypeStruct((b, l, d), F32),
            jax.ShapeDtypeStruct((b, SSM_COLS), F32),
            jax.ShapeDtypeStruct((b, SSM_COLS), F32),
            jax.ShapeDtypeStruct((b, CONV_PAD, CONV_WIDTH), F32),
        ],
        scratch_shapes=[
            pltpu.VMEM((2 * SSM_SLABS, BATCH_ROWS * (tm + SCAN_ROW_PAD), LANES), F32),
            pltpu.VMEM((SSM_SLABS, BATCH_ROWS, LANES), F32),
            pltpu.VMEM((SSM_SLABS, BATCH_ROWS, LANES), F32),
            pltpu.VMEM((BATCH_ROWS, CONV_PAD + tm, CONV_WIDTH), F32),
            pltpu.VMEM((BATCH_ROWS, tm, CONV_WIDTH), F32),
        ],
        compiler_params=pltpu.CompilerParams(
            dimension_semantics=("arbitrary", "arbitrary"), vmem_limit_bytes=VMEM_LIMIT),
        name="odd_mixer",
    )(x, p["norm1_mix"], p["w_in1"], bbd, lre, lim, cbd, p["ssm_d"], p["ssm_w_glu"], conv_w, p["conv_b"],
      p["conv_ln_g"], p["conv_ln_b"], p["w_out1"], s0re, s0im, conv0)
    shape3 = (b, SSM_GROUPS, SSM_STATE)
    return out, sre.reshape(shape3), sim.reshape(shape3), convn[:, CONV_PAD - (CONV_K - 1):, :]


def _ffn_kernel(x_ref, g_ref, wg_ref, wu_ref, wd_ref, o_ref, xn_ref):
    j = pl.program_id(1)

    @pl.when(j == 0)
    def _():
        x = x_ref[...]
        xn_ref[...] = _rms_norm(x, g_ref[...]).astype(BF16)
        o_ref[...] = x

    xn = xn_ref[...]
    hidden = _silu(_dot(xn, wg_ref[...])) * _dot(xn, wu_ref[...])
    o_ref[...] += _dot(hidden.astype(BF16), wd_ref[...])


def _dense_ffn(x, g, wg, wu, wd, *, tm, tf):
    n, d = x.shape
    f = wg.shape[1]
    x_spec = pl.BlockSpec((tm, d), lambda i, j: (i, 0))
    return pl.pallas_call(
        _ffn_kernel,
        grid=(n // tm, f // tf),
        in_specs=[
            x_spec, pl.BlockSpec((1, d), lambda i, j: (0, 0)),
            pl.BlockSpec((d, tf), lambda i, j: (0, j)), pl.BlockSpec((d, tf), lambda i, j: (0, j)),
            pl.BlockSpec((tf, d), lambda i, j: (j, 0)),
        ],
        out_specs=x_spec,
        out_shape=jax.ShapeDtypeStruct((n, d), F32),
        scratch_shapes=[pltpu.VMEM((tm, d), BF16)],
        compiler_params=pltpu.CompilerParams(
            dimension_semantics=("arbitrary", "arbitrary"), vmem_limit_bytes=VMEM_LIMIT),
        name="dense_ffn",
    )(x, g, wg, wu, wd)


ROUTE_GATE_LANE = 0
ROUTE_INDEX_LANE = 2


def _route_kernel(x_ref, g_ref, r_hi_ref, r_lo_ref, o_ref):
    xn = _rms_norm(x_ref[...], g_ref[...])
    logits = _split_dot(xn, r_hi_ref[...]) + _dot(xn.astype(BF16), r_lo_ref[...])
    lane = lax.broadcasted_iota(jnp.int32, logits.shape, 1)
    logits = jnp.where(lane < N_EXPERTS, logits, -jnp.inf)
    v1 = jnp.max(logits, axis=-1, keepdims=True)
    i1 = jnp.min(jnp.where(logits == v1, lane, LANES), axis=-1, keepdims=True)
    rest = jnp.where(lane == i1, -jnp.inf, logits)
    v2 = jnp.max(rest, axis=-1, keepdims=True)
    i2 = jnp.min(jnp.where(rest == v2, lane, LANES), axis=-1, keepdims=True)
    e2 = jnp.exp(v2 - v1)
    den = 1.0 + e2
    rec = jnp.where(lane == ROUTE_GATE_LANE, 1.0 / den, 0.0)
    rec = jnp.where(lane == ROUTE_GATE_LANE + 1, e2 / den, rec)
    rec = jnp.where(lane == ROUTE_INDEX_LANE, i1.astype(F32), rec)
    rec = jnp.where(lane == ROUTE_INDEX_LANE + 1, i2.astype(F32), rec)
    o_ref[...] = rec


def _gather_rows(src_hbm, dst_ref, index_ref, sem, count):
    def issue(r, carry):
        pltpu.make_async_copy(src_hbm.at[pl.ds(index_ref[r], 1), :], dst_ref.at[pl.ds(r, 1), :], sem).start()
        return carry

    lax.fori_loop(0, count, issue, 0, unroll=8)
    pltpu.make_async_copy(src_hbm.at[pl.ds(0, count), :], dst_ref, sem).wait()


def _expert_kernel(texp_ref, used_ref, src_ref, x_hbm, g_ref, wg_ref, wu_ref, wd_ref, y_ref, xg_ref, xn_ref, sem,
                   *, tm):
    i = pl.program_id(0)
    j = pl.program_id(1)
    used = used_ref[i] > 0

    @pl.when(jnp.logical_and(j == 0, used))
    def _():
        _gather_rows(x_hbm, xg_ref, src_ref, sem, tm)
        xn_ref[...] = _rms_norm(xg_ref[...], g_ref[...]).astype(BF16)

    @pl.when(jnp.logical_and(j == 0, jnp.logical_not(used)))
    def _():
        y_ref[...] = jnp.zeros_like(y_ref)

    @pl.when(used)
    def _():
        xn = xn_ref[...]
        hidden = _silu(_dot(xn, wg_ref[...])) * _dot(xn, wu_ref[...])
        part = _dot(hidden.astype(BF16), wd_ref[...])

        @pl.when(j == 0)
        def _():
            y_ref[...] = part

        @pl.when(j > 0)
        def _():
            y_ref[...] += part


def _combine_kernel(pos0_ref, pos1_ref, route_ref, x_ref, y_hbm, o_ref, y0_ref, y1_ref, sem, *, tc):
    _gather_rows(y_hbm, y0_ref, pos0_ref, sem.at[0], tc)
    _gather_rows(y_hbm, y1_ref, pos1_ref, sem.at[1], tc)
    route = route_ref[...]
    g0 = route[:, ROUTE_GATE_LANE:ROUTE_GATE_LANE + 1]
    g1 = route[:, ROUTE_GATE_LANE + 1:ROUTE_GATE_LANE + 2]
    o_ref[...] = x_ref[...] + (g0 * y0_ref[...] + g1 * y1_ref[...])


def _dispatch_plan(experts, tm, n_tiles):
    n = experts.shape[0]
    flat = experts.reshape(-1)
    onehot = (flat[:, None] == jnp.arange(N_EXPERTS, dtype=jnp.int32)[None, :]).astype(jnp.int32)
    csum = jnp.cumsum(onehot, axis=0)
    rank = jnp.sum((csum - onehot) * onehot, axis=1)
    counts = csum[-1]
    tiles = (counts + tm - 1) // tm
    tile_end = jnp.cumsum(tiles)
    tile_start = tile_end - tiles
    pos = tile_start[flat] * tm + rank
    src = jnp.zeros((n_tiles * tm,), jnp.int32).at[pos].set(jnp.arange(2 * n, dtype=jnp.int32) // 2)
    tile_ids = jnp.arange(n_tiles, dtype=jnp.int32)
    used = (tile_ids < tile_end[-1]).astype(jnp.int32)
    tile_expert = jnp.minimum(jnp.searchsorted(tile_end, tile_ids, side="right"), N_EXPERTS - 1).astype(jnp.int32)
    last_expert = tile_expert[jnp.maximum(tile_end[-1] - 1, 0)]
    tile_expert = jnp.where(used > 0, tile_expert, last_expert)
    pos = pos.reshape(n, 2)
    return src, pos[:, 0], pos[:, 1], tile_expert, used


def _moe_ffn(x, g, r_hi, r_lo, wg, wu, wd, *, tr, tm, tf, tc):
    n, d = x.shape
    e, _, f = wg.shape
    tpe = f // tf
    n_tiles = (TOP_K * n) // tm + e
    smem = pltpu.SMEM

    route = pl.pallas_call(
        _route_kernel,
        grid=(n // tr,),
        in_specs=[
            pl.BlockSpec((tr, d), lambda i: (i, 0)), pl.BlockSpec((1, d), lambda i: (0, 0)),
            pl.BlockSpec((d, LANES), lambda i: (0, 0)), pl.BlockSpec((d, LANES), lambda i: (0, 0)),
        ],
        out_specs=pl.BlockSpec((tr, LANES), lambda i: (i, 0)),
        out_shape=jax.ShapeDtypeStruct((n, LANES), F32),
        compiler_params=pltpu.CompilerParams(dimension_semantics=("arbitrary",), vmem_limit_bytes=VMEM_LIMIT),
        name="moe_route",
    )(x, g, r_hi, r_lo)

    experts = route[:, ROUTE_INDEX_LANE:ROUTE_INDEX_LANE + TOP_K].astype(jnp.int32)
    src, pos0, pos1, tile_expert, used = _dispatch_plan(experts, tm, n_tiles)

    def ff_tile(i, j, texp, used):
        return jnp.where(used[i] > 0, j, tpe - 1)

    y = pl.pallas_call(
        functools.partial(_expert_kernel, tm=tm),
        grid_spec=pltpu.PrefetchScalarGridSpec(
            num_scalar_prefetch=2,
            grid=(n_tiles, tpe),
            in_specs=[
                pl.BlockSpec((tm,), lambda i, j, texp, used: (i,), memory_space=smem),
                pl.BlockSpec(memory_space=pl.ANY),
                pl.BlockSpec((1, d), lambda i, j, texp, used: (0, 0)),
                pl.BlockSpec((None, d, tf), lambda i, j, texp, used: (texp[i], 0, ff_tile(i, j, texp, used))),
                pl.BlockSpec((None, d, tf), lambda i, j, texp, used: (texp[i], 0, ff_tile(i, j, texp, used))),
                pl.BlockSpec((None, tf, d), lambda i, j, texp, used: (texp[i], ff_tile(i, j, texp, used), 0)),
            ],
            out_specs=pl.BlockSpec((tm, d), lambda i, j, texp, used: (i, 0)),
            scratch_shapes=[pltpu.VMEM((tm, d), F32), pltpu.VMEM((tm, d), BF16), pltpu.SemaphoreType.DMA(())],
        ),
        out_shape=jax.ShapeDtypeStruct((n_tiles * tm, d), F32),
        compiler_params=pltpu.CompilerParams(
            dimension_semantics=("arbitrary", "arbitrary"), vmem_limit_bytes=VMEM_LIMIT),
        name="moe_experts",
    )(tile_expert, used, src, x, g, wg, wu, wd)

    return pl.pallas_call(
        functools.partial(_combine_kernel, tc=tc),
        grid=(n // tc,),
        in_specs=[
            pl.BlockSpec((tc,), lambda i: (i,), memory_space=smem),
            pl.BlockSpec((tc,), lambda i: (i,), memory_space=smem),
            pl.BlockSpec((tc, LANES), lambda i: (i, 0)),
            pl.BlockSpec((tc, d), lambda i: (i, 0)),
            pl.BlockSpec(memory_space=pl.ANY),
        ],
        out_specs=pl.BlockSpec((tc, d), lambda i: (i, 0)),
        out_shape=jax.ShapeDtypeStruct((n, d), F32),
        scratch_shapes=[pltpu.VMEM((tc, d), F32), pltpu.VMEM((tc, d), F32), pltpu.SemaphoreType.DMA((2,))],
        compiler_params=pltpu.CompilerParams(dimension_semantics=("arbitrary",), vmem_limit_bytes=VMEM_LIMIT),
        name="moe_combine",
    )(pos0, pos1, route, x, y)


def kernel(x_prompt, x_sample, cache_k_swa, cache_v_swa, state_ret, state_ssm_re, state_ssm_im, state_conv,
           norm0_mix, w_in0, ret_gn, swa_qnorm, swa_knorm, swa_sinks, w_out0, norm0_ffn, ffn0_w_gate,
           ffn0_w_up, ffn0_w_down, norm1_mix, w_in1, ssm_a_re, ssm_a_im, ssm_log_dt, ssm_b_re, ssm_b_im,
           ssm_c_re, ssm_c_im, ssm_d, ssm_w_glu, conv_w, conv_b, conv_ln_g, conv_ln_b, w_out1, norm1_ffn,
           moe_router, moe_w_gate, moe_w_up, moe_w_down):
    bp, lp, d = x_prompt.shape
    bs, ls, _ = x_sample.shape
    row = lambda v: v.astype(F32).reshape(1, -1)
    r_pad = jnp.pad(moe_router.astype(F32), ((0, 0), (0, LANES - N_EXPERTS)))
    r_hi = r_pad.astype(BF16)
    p = dict(
        sinks=swa_sinks.astype(F32), norm0_mix=row(norm0_mix), w_in0=w_in0.astype(BF16), w_out0=w_out0.astype(BF16),
        ret_gn=row(ret_gn), qnorm=row(jnp.tile(swa_qnorm, SWA_HEADS)), knorm=row(jnp.tile(swa_knorm, SWA_KV_HEADS)),
        norm1_mix=row(norm1_mix), w_in1=w_in1.astype(BF16), ssm_a_re=ssm_a_re, ssm_a_im=ssm_a_im,
        ssm_log_dt=ssm_log_dt, ssm_b_re=ssm_b_re, ssm_b_im=ssm_b_im, ssm_c_re=ssm_c_re, ssm_c_im=ssm_c_im,
        ssm_d=row(ssm_d), ssm_w_glu=ssm_w_glu.astype(BF16), conv_w=conv_w.astype(F32), conv_b=row(conv_b),
        conv_ln_g=row(conv_ln_g), conv_ln_b=row(conv_ln_b), w_out1=w_out1.astype(BF16),
    )
    ffn0 = (row(norm0_ffn), ffn0_w_gate.astype(BF16), ffn0_w_up.astype(BF16), ffn0_w_down.astype(BF16))
    moe = (row(norm1_ffn), r_hi, (r_pad - r_hi.astype(F32)).astype(BF16),
           moe_w_gate.astype(BF16), moe_w_up.astype(BF16), moe_w_down.astype(BF16))

    pos_p = jnp.arange(lp, dtype=jnp.int32)
    pos_s = PAST_LEN + jnp.arange(ls, dtype=jnp.int32)
    zeros_kv = jnp.zeros((bp, WINDOW, LANES), F32)
    zeros_ret = jnp.zeros((bp, RET_HEADS, RET_DK, RET_DV), F32)
    zeros_ssm = jnp.zeros((bp, SSM_COLS), F32)
    zeros_conv = jnp.zeros((bp, CONV_K - 1, CONV_WIDTH), F32)
    win = cache_k_swa.shape[1]

    h_p, k_p, v_p, ret_p = _even_mixer(x_prompt, pos_p, zeros_kv, zeros_kv, zeros_ret, True, p)
    h_s, k_s, v_s, ret_s = _even_mixer(
        x_sample, pos_s, cache_k_swa.reshape(bs, win, LANES), cache_v_swa.reshape(bs, win, LANES), state_ret,
        False, p)

    h_p = _dense_ffn(h_p.reshape(bp * lp, d), *ffn0, tm=1024, tf=1408).reshape(bp, lp, d)
    h_s = _dense_ffn(h_s.reshape(bs * ls, d), *ffn0, tm=1024, tf=1408).reshape(bs, ls, d)

    h_p, sre_p, sim_p, conv_p = _odd_mixer(h_p, zeros_ssm, zeros_ssm, zeros_conv, p)
    h_s, sre_s, sim_s, conv_s = _odd_mixer(
        h_s, state_ssm_re.reshape(bs, SSM_COLS), state_ssm_im.reshape(bs, SSM_COLS), state_conv, p)

    h_p = _moe_ffn(h_p.reshape(bp * lp, d), *moe, tr=1024, tm=1024, tf=512, tc=512).reshape(bp, lp, d)
    h_s = _moe_ffn(h_s.reshape(bs * ls, d), *moe, tr=1024, tm=512, tf=512, tc=512).reshape(bs, ls, d)

    kv_p = (bp, WINDOW, SWA_KV_HEADS, SWA_DH)
    kv_s = (bs, win, SWA_KV_HEADS, SWA_DH)
    return (h_p, h_s, k_p.reshape(kv_p), v_p.reshape(kv_p), k_s.reshape(kv_s), v_s.reshape(kv_s), ret_p, ret_s,
            sre_p, sim_p, sre_s, sim_s, conv_p, conv_s)
```

```python
import functools
import math

import jax
import jax.numpy as jnp
import numpy as np
from jax import lax
from jax.experimental import pallas as pl
from jax.experimental.pallas import tpu as pltpu

F32 = jnp.float32
BF16 = jnp.bfloat16

D_MODEL = 1024
CHUNK = 64
EPS = 1e-6
ROPE_THETA = 10000.0
PAST_LEN = 4096
RET_HEADS, RET_DK, RET_DV = 4, 64, 128
SWA_HEADS, SWA_KV_HEADS, SWA_DH = 8, 2, 64
SWA_GROUP = SWA_HEADS // SWA_KV_HEADS
WINDOW = 128
SSM_WIDTH = D_MODEL // 2
SSM_GROUP = 16
SSM_GROUPS = SSM_WIDTH // SSM_GROUP
SSM_STATE = 64
SSM_COLS = SSM_GROUPS * SSM_STATE
CONV_WIDTH = D_MODEL // 2
CONV_K = 31
CONV_PAD = 32
N_EXPERTS = 8
TOP_K = 2

QA0, KA0, VA0, GA0 = 0, 256, 512, 1024
QB0, KB0, VB0 = 1536, 2048, 2176
IN0_COLS = 2304
OUT0_COLS = 1024
IN1_COLS = 1536

LANES = 128
VMEM_LIMIT = 56 * 1024 * 1024


def _rms_norm(x, g):
    return x * lax.rsqrt(jnp.mean(x * x, axis=-1, keepdims=True) + EPS) * g


def _silu(x):
    return x * jax.nn.sigmoid(x)


def _dot(a, b):
    return jnp.dot(a, b, preferred_element_type=F32)


def _dot_nt(a, b):
    return lax.dot_general(a, b, (((1,), (1,)), ((), ())), preferred_element_type=F32)


def _dot_tn(a, b):
    return lax.dot_general(a, b, (((0,), (0,)), ((), ())), preferred_element_type=F32)


def _rot_half(x):
    w = x.shape[-1]
    lane = lax.broadcasted_iota(jnp.int32, x.shape, x.ndim - 1)
    fwd = pltpu.roll(x, w - SWA_DH // 2, x.ndim - 1)
    bwd = pltpu.roll(x, SWA_DH // 2, x.ndim - 1)
    return jnp.where(lane % SWA_DH < SWA_DH // 2, fwd, bwd)


def _rope(x, cos, sin_signed):
    reps = x.shape[-1] // LANES
    if reps > 1:
        cos = jnp.concatenate([cos] * reps, axis=-1)
        sin_signed = jnp.concatenate([sin_signed] * reps, axis=-1)
    return x * cos + _rot_half(x) * sin_signed


def _split_dot(x, w_bf16):
    hi = x.astype(BF16)
    lo = (x - hi.astype(F32)).astype(BF16)
    return _dot(hi, w_bf16) + _dot(lo, w_bf16)


def _even_kernel(sink_ref, x_ref, g_ref, win_ref, wout_ref, cos_ref, sin_ref, intra_ref, qdec_ref, kdec_ref,
                 gn_ref, qn_ref, kn_ref, ones_ref, kprev_ref, vprev_ref, s0_ref,
                 o_ref, knew_ref, vnew_ref, sfin_ref,
                 z_ref, mix_ref, kwin_ref, vwin_ref, s_ref, *, chunks, mask_initial, chunk_decay):
    t = pl.program_id(1)

    @pl.when(t == 0)
    def _():
        kwin_ref[0:WINDOW, :] = kprev_ref[...]
        vwin_ref[0:WINDOW, :] = vprev_ref[...]
        s_ref[...] = s0_ref[...]

    x = x_ref[...]
    xn = _rms_norm(x, g_ref[...]).astype(BF16)
    z_ref[...] = _dot(xn, win_ref[...])

    cos = cos_ref[...]
    sin = sin_ref[...]
    z_ref[:, QA0:KA0] = _rope(z_ref[:, QA0:KA0], cos, sin)
    z_ref[:, KA0:VA0] = _rope(z_ref[:, KA0:VA0], cos, sin) * (RET_DK ** -0.5)
    qk = z_ref[:, QB0:VB0]
    ms = _split_dot(qk * qk, ones_ref[...]) * (1.0 / SWA_DH)
    qk = qk * lax.rsqrt(ms + EPS)
    nq = SWA_HEADS * SWA_DH
    z_ref[:, QB0:KB0] = _rope(qk[:, :nq] * qn_ref[...], cos, sin)
    z_ref[:, KB0:VB0] = _rope(qk[:, nq:] * kn_ref[...], cos, sin)

    row = lax.broadcasted_iota(jnp.int32, (SWA_GROUP * CHUNK, 1), 0)
    key_pos = lax.broadcasted_iota(jnp.int32, (1, WINDOW + CHUNK), 1)

    def chunk_step(c, carry):
        r0 = pl.multiple_of(c * CHUNK, CHUNK)
        rows = pl.ds(r0, CHUNK)
        for h in range(RET_HEADS):
            q = z_ref[rows, QA0 + h * RET_DK:QA0 + (h + 1) * RET_DK]
            k = z_ref[rows, KA0 + h * RET_DK:KA0 + (h + 1) * RET_DK]
            v = z_ref[rows, VA0 + h * RET_DV:VA0 + (h + 1) * RET_DV].astype(BF16)
            gate = z_ref[rows, GA0 + h * RET_DV:GA0 + (h + 1) * RET_DV]
            qd = q * qdec_ref[:, h * RET_DK:(h + 1) * RET_DK]
            kd = k * kdec_ref[:, h * RET_DK:(h + 1) * RET_DK]
            scores = _dot_nt(q.astype(BF16), k.astype(BF16)) * intra_ref[h]
            state = s_ref[h]
            o = _dot(scores.astype(BF16), v) + _dot(qd.astype(BF16), state.astype(BF16))
            s_ref[h] = chunk_decay[h] * state + _dot_tn(kd.astype(BF16), v)
            oc = o - jnp.mean(o, axis=-1, keepdims=True)
            on = oc * lax.rsqrt(jnp.mean(oc * oc, axis=-1, keepdims=True) + EPS)
            on = on * gn_ref[:, h * RET_DV:(h + 1) * RET_DV]
            mix_ref[rows, h * RET_DV:(h + 1) * RET_DV] = (_silu(gate) * on).astype(BF16)
        kwin_ref[WINDOW:, :] = z_ref[rows, KB0:VB0]
        vwin_ref[WINDOW:, :] = z_ref[rows, VB0:IN0_COLS]
        if mask_initial:
            first_valid = WINDOW - (t * chunks + c) * CHUNK
        for j in range(SWA_KV_HEADS):
            keys = kwin_ref[:, j * SWA_DH:(j + 1) * SWA_DH].astype(BF16)
            vals = vwin_ref[:, j * SWA_DH:(j + 1) * SWA_DH].astype(BF16)
            heads = [SWA_GROUP * j + g for g in range(SWA_GROUP)]
            qg = jnp.concatenate(
                [z_ref[rows, QB0 + hh * SWA_DH:QB0 + (hh + 1) * SWA_DH] for hh in heads], axis=0)
            s = _dot_nt(qg.astype(BF16), keys) * (SWA_DH ** -0.5)
            if mask_initial:
                s = jnp.where(key_pos >= first_valid, s, -1e30)
            sink = jnp.zeros((SWA_GROUP * CHUNK, 1), F32)
            for g, hh in enumerate(heads):
                sink = jnp.where(row // CHUNK == g, sink_ref[hh], sink)
            m = jnp.maximum(jnp.max(s, axis=-1, keepdims=True), sink)
            p = jnp.exp(s - m)
            den = jnp.sum(p, axis=-1, keepdims=True) + jnp.exp(sink - m)
            o = _dot(p.astype(BF16), vals) / den
            for g, hh in enumerate(heads):
                c0 = RET_HEADS * RET_DV + hh * SWA_DH
                mix_ref[rows, c0:c0 + SWA_DH] = o[g * CHUNK:(g + 1) * CHUNK, :].astype(BF16)
        kwin_ref[0:WINDOW, :] = kwin_ref[CHUNK:, :]
        vwin_ref[0:WINDOW, :] = vwin_ref[CHUNK:, :]
        return carry

    lax.fori_loop(0, chunks, chunk_step, 0, unroll=min(4, chunks))

    o_ref[...] = x + _dot(mix_ref[...], wout_ref[...])
    knew_ref[...] = kwin_ref[0:WINDOW, :]
    vnew_ref[...] = vwin_ref[0:WINDOW, :]
    sfin_ref[...] = s_ref[...]


def _retention_tables():
    h = np.arange(RET_HEADS, dtype=np.float64)
    log_g = np.log1p(-np.exp2(-5.0 - h))
    idx = np.arange(CHUNK, dtype=np.float64)
    intra = np.exp(np.abs(idx[:, None] - idx[None, :])[None] * log_g[:, None, None])
    q_dec = np.exp((idx + 1.0)[:, None] * log_g[None, :])
    k_dec = np.exp((CHUNK - 1.0 - idx)[:, None] * log_g[None, :])
    chunk_dec = tuple(float(v) for v in np.exp(CHUNK * log_g))
    q_dec = np.repeat(q_dec, RET_DK, axis=1)
    k_dec = np.repeat(k_dec, RET_DK, axis=1)
    return (jnp.asarray(intra, F32), jnp.asarray(q_dec, F32), jnp.asarray(k_dec, F32), chunk_dec)


def _rope_tables(pos):
    half = SWA_DH // 2
    inv = jnp.exp(-math.log(ROPE_THETA) * jnp.arange(half, dtype=F32) / half)
    ang = pos.astype(F32)[:, None] * inv[None, :]
    cos, sin = jnp.cos(ang), jnp.sin(ang)
    cos = jnp.concatenate([cos, cos] * (LANES // SWA_DH), axis=-1)
    sin = jnp.concatenate([-sin, sin] * (LANES // SWA_DH), axis=-1)
    return cos, sin


def _even_mixer(x, pos, kprev, vprev, s0, mask_initial, p):
    b, l, d = x.shape
    tq = min(512, l)
    chunks = tq // CHUNK
    intra, q_dec, k_dec, chunk_decay = _retention_tables()
    cos, sin = _rope_tables(pos)
    nqk = (SWA_HEADS + SWA_KV_HEADS) * SWA_DH
    group = np.arange(nqk) // SWA_DH
    ones_blk = jnp.asarray(group[:, None] == group[None, :], BF16)

    def const(shape):
        return pl.BlockSpec(shape, lambda bi, ti: (0,) * len(shape))

    kv_spec = pl.BlockSpec((None, WINDOW, LANES), lambda bi, ti: (bi, 0, 0))
    st_spec = pl.BlockSpec((None, RET_HEADS, RET_DK, RET_DV), lambda bi, ti: (bi, 0, 0, 0))
    x_spec = pl.BlockSpec((None, tq, d), lambda bi, ti: (bi, ti, 0))
    tab_spec = pl.BlockSpec((tq, LANES), lambda bi, ti: (ti, 0))
    kern = functools.partial(_even_kernel, chunks=chunks, mask_initial=mask_initial, chunk_decay=chunk_decay)
    return pl.pallas_call(
        kern,
        grid=(b, l // tq),
        in_specs=[
            pl.BlockSpec(memory_space=pltpu.SMEM),
            x_spec, const((1, d)), const((d, IN0_COLS)), const((OUT0_COLS, d)),
            tab_spec, tab_spec,
            const((RET_HEADS, CHUNK, CHUNK)), const((CHUNK, RET_HEADS * RET_DK)), const((CHUNK, RET_HEADS * RET_DK)),
            const((1, RET_HEADS * RET_DV)), const((1, SWA_HEADS * SWA_DH)), const((1, SWA_KV_HEADS * SWA_DH)),
            const((nqk, nqk)), kv_spec, kv_spec, st_spec,
        ],
        out_specs=[x_spec, kv_spec, kv_spec, st_spec],
        out_shape=[
            jax.ShapeDtypeStruct((b, l, d), F32),
            jax.ShapeDtypeStruct((b, WINDOW, LANES), F32),
            jax.ShapeDtypeStruct((b, WINDOW, LANES), F32),
            jax.ShapeDtypeStruct((b, RET_HEADS, RET_DK, RET_DV), F32),
        ],
        scratch_shapes=[
            pltpu.VMEM((tq, IN0_COLS), F32),
            pltpu.VMEM((tq, OUT0_COLS), BF16),
            pltpu.VMEM((WINDOW + CHUNK, LANES), F32),
            pltpu.VMEM((WINDOW + CHUNK, LANES), F32),
            pltpu.VMEM((RET_HEADS, RET_DK, RET_DV), F32),
        ],
        compiler_params=pltpu.CompilerParams(
            dimension_semantics=("arbitrary", "arbitrary"), vmem_limit_bytes=VMEM_LIMIT),
        name="even_mixer",
    )(p["sinks"], x, p["norm0_mix"], p["w_in0"], p["w_out0"], cos, sin, intra, q_dec, k_dec,
      p["ret_gn"], p["qnorm"], p["knorm"], ones_blk, kprev, vprev, s0)


SSM_SLABS = SSM_COLS // LANES
SCAN_SLABS = 4
BATCH_ROWS = 8
SCAN_ROW_PAD = 8


def _odd_kernel(x_ref, g_ref, win_ref, bbd_ref, lre_ref, lim_ref, cbd_ref, d_ref, wglu_ref, cw_ref, cb_ref,
                lng_ref, lnb_ref, wout_ref, s0re_ref, s0im_ref, conv0_ref,
                o_ref, sre_ref, sim_ref, convn_ref,
                bu_ref, hre_ref, him_ref, xp_ref, cv_ref, *, tm):
    t = pl.program_id(1)
    rows = BATCH_ROWS * tm

    @pl.when(t == 0)
    def _():
        for j in range(SSM_SLABS):
            hre_ref[j] = s0re_ref[:, j * LANES:(j + 1) * LANES]
            him_ref[j] = s0im_ref[:, j * LANES:(j + 1) * LANES]
        xp_ref[:, 0:CONV_PAD, :] = conv0_ref[...]

    x = x_ref[...].reshape(rows, D_MODEL)
    xn = _rms_norm(x, g_ref[...]).astype(BF16)
    z = _dot(xn, win_ref[...])
    u = z[:, 0:SSM_WIDTH]
    d_val = z[:, SSM_WIDTH:SSM_WIDTH + CONV_WIDTH]
    d_gate = z[:, SSM_WIDTH + CONV_WIDTH:]

    pitch = tm + SCAN_ROW_PAD
    bu = _dot(u.astype(BF16), bbd_ref[...])
    for j in range(2 * SSM_SLABS):
        for b in range(BATCH_ROWS):
            bu_ref[j, b * pitch:b * pitch + tm, :] = bu[b * tm:(b + 1) * tm, j * LANES:(j + 1) * LANES]
    for j0 in range(0, SSM_SLABS, SCAN_SLABS):
        slabs = range(j0, j0 + SCAN_SLABS)
        lr = [lre_ref[j] for j in slabs]
        li = [lim_ref[j] for j in slabs]

        def step(ti, carry):
            at_t = pl.ds(ti, BATCH_ROWS, stride=pitch)
            out = []
            for n, j in enumerate(slabs):
                hr, hi = carry[n]
                nr = lr[n] * hr - li[n] * hi + bu_ref[j, at_t, :]
                ni = lr[n] * hi + li[n] * hr + bu_ref[SSM_SLABS + j, at_t, :]
                bu_ref[j, at_t, :] = nr
                bu_ref[SSM_SLABS + j, at_t, :] = ni
                out.append((nr, ni))
            return tuple(out)

        init = tuple((hre_ref[j], him_ref[j]) for j in slabs)
        final = lax.fori_loop(0, tm, step, init, unroll=4)
        for n, j in enumerate(slabs):
            hre_ref[j], him_ref[j] = final[n]

    def slab_states(j):
        return jnp.concatenate([bu_ref[j, b * pitch:b * pitch + tm, :] for b in range(BATCH_ROWS)], axis=0)

    states = jnp.concatenate([slab_states(j).astype(BF16) for j in range(2 * SSM_SLABS)], axis=-1)
    y = _dot(states, cbd_ref[...]) + d_ref[...] * u
    g = 0.5 * y * (1.0 + lax.erf(y * (2.0 ** -0.5)))
    o_c = g * jax.nn.sigmoid(_dot(g.astype(BF16), wglu_ref[...]))

    u_d = d_val * jax.nn.sigmoid(d_gate)
    xp_ref[:, CONV_PAD:, :] = u_d.reshape(BATCH_ROWS, tm, CONV_WIDTH)
    first = CONV_PAD - (CONV_K - 1)
    sub = 8

    def conv_sequence(b, carry):
        for c0 in range(0, CONV_WIDTH, LANES):
            cols = slice(c0, c0 + LANES)
            blk = jnp.zeros((tm, LANES), F32) + cb_ref[:, cols]
            for r in range(sub):
                taps = range(r, CONV_K, sub)
                shifted = xp_ref[b, first + r:first + r + tm + sub * (len(taps) - 1), cols]
                for q, k in enumerate(taps):
                    blk = blk + shifted[sub * q:sub * q + tm, :] * cw_ref[k:k + 1, cols]
            cv_ref[b, :, cols] = blk
        return carry

    lax.fori_loop(0, BATCH_ROWS, conv_sequence, 0)
    acc = cv_ref[...]
    mu = jnp.mean(acc, axis=-1, keepdims=True)
    xc = acc - mu
    var = jnp.mean(xc * xc, axis=-1, keepdims=True)
    o_d = _silu(xc * lax.rsqrt(var + EPS) * lng_ref[...] + lnb_ref[...]).reshape(rows, CONV_WIDTH)
    xp_ref[:, 0:CONV_PAD, :] = xp_ref[:, tm:tm + CONV_PAD, :]

    mix = jnp.concatenate([o_c.astype(BF16), o_d.astype(BF16)], axis=-1)
    o_ref[...] = (x + _dot(mix, wout_ref[...])).reshape(BATCH_ROWS, tm, D_MODEL)
    for j in range(SSM_SLABS):
        sre_ref[:, j * LANES:(j + 1) * LANES] = hre_ref[j]
        sim_ref[:, j * LANES:(j + 1) * LANES] = him_ref[j]
    convn_ref[...] = xp_ref[:, 0:CONV_PAD, :]


def _s5_discretize(p):
    dt = jnp.exp(p["ssm_log_dt"].astype(F32))[:, None]
    are, aim = p["ssm_a_re"].astype(F32), p["ssm_a_im"].astype(F32)
    mag = jnp.exp(are * dt)
    lb_re, lb_im = mag * jnp.cos(aim * dt), mag * jnp.sin(aim * dt)
    den = are * are + aim * aim
    nr, ni = lb_re - 1.0, lb_im
    f_re = (nr * are + ni * aim) / den
    f_im = (ni * are - nr * aim) / den
    br, bi = p["ssm_b_re"].astype(F32), p["ssm_b_im"].astype(F32)
    bb_re = f_re[..., None] * br - f_im[..., None] * bi
    bb_im = f_re[..., None] * bi + f_im[..., None] * br
    return lb_re, lb_im, bb_re, bb_im


def _s5_operands(p):
    lb_re, lb_im, bb_re, bb_im = _s5_discretize(p)
    eye = jnp.eye(SSM_GROUPS, dtype=F32)

    def in_map(bb):
        return jnp.einsum("gpc,gh->gchp", bb, eye).reshape(SSM_WIDTH, SSM_COLS)

    def out_map(c):
        return jnp.einsum("gcp,gh->gphc", c.astype(F32), eye).reshape(SSM_COLS, SSM_WIDTH)

    bbd = jnp.concatenate([in_map(bb_re), in_map(bb_im)], axis=1).astype(BF16)
    cbd = jnp.concatenate([out_map(p["ssm_c_re"]), -out_map(p["ssm_c_im"])], axis=0).astype(BF16)
    slab_shape = (SSM_SLABS, BATCH_ROWS, LANES)
    lre = jnp.broadcast_to(lb_re.reshape(SSM_SLABS, 1, LANES), slab_shape)
    lim = jnp.broadcast_to(lb_im.reshape(SSM_SLABS, 1, LANES), slab_shape)
    return bbd, cbd, lre, lim


def _odd_mixer(x, s0re, s0im, conv0, p):
    b, l, d = x.shape
    tm = min(64, l)
    bbd, cbd, lre, lim = _s5_operands(p)
    conv0 = jnp.pad(conv0, ((0, 0), (CONV_PAD - (CONV_K - 1), 0), (0, 0)))
    conv_w = jnp.pad(p["conv_w"], ((0, CONV_PAD - CONV_K), (0, 0)))

    def const(shape):
        return pl.BlockSpec(shape, lambda bi, ti: (0,) * len(shape))

    x_spec = pl.BlockSpec((BATCH_ROWS, tm, d), lambda bi, ti: (bi, ti, 0))
    st_spec = pl.BlockSpec((BATCH_ROWS, SSM_COLS), lambda bi, ti: (bi, 0))
    cv_spec = pl.BlockSpec((BATCH_ROWS, CONV_PAD, CONV_WIDTH), lambda bi, ti: (bi, 0, 0))
    out, sre, sim, convn = pl.pallas_call(
        functools.partial(_odd_kernel, tm=tm),
        grid=(b // BATCH_ROWS, l // tm),
        in_specs=[
            x_spec, const((1, d)), const((d, IN1_COLS)), const((SSM_WIDTH, 2 * SSM_COLS)),
            const((SSM_SLABS, BATCH_ROWS, LANES)), const((SSM_SLABS, BATCH_ROWS, LANES)),
            const((2 * SSM_COLS, SSM_WIDTH)),
            const((1, SSM_WIDTH)), const((SSM_WIDTH, SSM_WIDTH)), const((CONV_PAD, CONV_WIDTH)),
            const((1, CONV_WIDTH)), const((1, CONV_WIDTH)), const((1, CONV_WIDTH)),
            const((SSM_WIDTH + CONV_WIDTH, d)), st_spec, st_spec, cv_spec,
        ],
        out_specs=[x_spec, st_spec, st_spec, cv_spec],
        out_shape=[
            jax.ShapeDtypeStruct((b, l, d), F32),
            jax.ShapeDtypeStruct((b, SSM_COLS), F32),
            jax.ShapeDtypeStruct((b, SSM_COLS), F32),
            jax.ShapeDtypeStruct((b, CONV_PAD, CONV_WIDTH), F32),
        ],
        scratch_shapes=[
            pltpu.VMEM((2 * SSM_SLABS, BATCH_ROWS * (tm + SCAN_ROW_PAD), LANES), F32),
            pltpu.VMEM((SSM_SLABS, BATCH_ROWS, LANES), F32),
            pltpu.VMEM((SSM_SLABS, BATCH_ROWS, LANES), F32),
            pltpu.VMEM((BATCH_ROWS, CONV_PAD + tm, CONV_WIDTH), F32),
            pltpu.VMEM((BATCH_ROWS, tm, CONV_WIDTH), F32),
        ],
        compiler_params=pltpu.CompilerParams(
            dimension_semantics=("arbitrary", "arbitrary"), vmem_limit_bytes=VMEM_LIMIT),
        name="odd_mixer",
    )(x, p["norm1_mix"], p["w_in1"], bbd, lre, lim, cbd, p["ssm_d"], p["ssm_w_glu"], conv_w, p["conv_b"],
      p["conv_ln_g"], p["conv_ln_b"], p["w_out1"], s0re, s0im, conv0)
    shape3 = (b, SSM_GROUPS, SSM_STATE)
    return out, sre.reshape(shape3), sim.reshape(shape3), convn[:, CONV_PAD - (CONV_K - 1):, :]


def _ffn_kernel(x_ref, g_ref, wg_ref, wu_ref, wd_ref, o_ref, xn_ref):
    j = pl.program_id(1)

    @pl.when(j == 0)
    def _():
        x = x_ref[...]
        xn_ref[...] = _rms_norm(x, g_ref[...]).astype(BF16)
        o_ref[...] = x

    xn = xn_ref[...]
    hidden = _silu(_dot(xn, wg_ref[...])) * _dot(xn, wu_ref[...])
    o_ref[...] += _dot(hidden.astype(BF16), wd_ref[...])


def _dense_ffn(x, g, wg, wu, wd, *, tm, tf):
    n, d = x.shape
    f = wg.shape[1]
    x_spec = pl.BlockSpec((tm, d), lambda i, j: (i, 0))
    return pl.pallas_call(
        _ffn_kernel,
        grid=(n // tm, f // tf),
        in_specs=[
            x_spec, pl.BlockSpec((1, d), lambda i, j: (0, 0)),
            pl.BlockSpec((d, tf), lambda i, j: (0, j)), pl.BlockSpec((d, tf), lambda i, j: (0, j)),
            pl.BlockSpec((tf, d), lambda i, j: (j, 0)),
        ],
        out_specs=x_spec,
        out_shape=jax.ShapeDtypeStruct((n, d), F32),
        scratch_shapes=[pltpu.VMEM((tm, d), BF16)],
        compiler_params=pltpu.CompilerParams(
            dimension_semantics=("arbitrary", "arbitrary"), vmem_limit_bytes=VMEM_LIMIT),
        name="dense_ffn",
    )(x, g, wg, wu, wd)


ROUTE_GATE_LANE = 0
ROUTE_INDEX_LANE = 2


def _route_kernel(x_ref, g_ref, r_hi_ref, r_lo_ref, o_ref):
    xn = _rms_norm(x_ref[...], g_ref[...])
    logits = _split_dot(xn, r_hi_ref[...]) + _dot(xn.astype(BF16), r_lo_ref[...])
    lane = lax.broadcasted_iota(jnp.int32, logits.shape, 1)
    logits = jnp.where(lane < N_EXPERTS, logits, -jnp.inf)
    v1 = jnp.max(logits, axis=-1, keepdims=True)
    i1 = jnp.min(jnp.where(logits == v1, lane, LANES), axis=-1, keepdims=True)
    rest = jnp.where(lane == i1, -jnp.inf, logits)
    v2 = jnp.max(rest, axis=-1, keepdims=True)
    i2 = jnp.min(jnp.where(rest == v2, lane, LANES), axis=-1, keepdims=True)
    e2 = jnp.exp(v2 - v1)
    den = 1.0 + e2
    rec = jnp.where(lane == ROUTE_GATE_LANE, 1.0 / den, 0.0)
    rec = jnp.where(lane == ROUTE_GATE_LANE + 1, e2 / den, rec)
    rec = jnp.where(lane == ROUTE_INDEX_LANE, i1.astype(F32), rec)
    rec = jnp.where(lane == ROUTE_INDEX_LANE + 1, i2.astype(F32), rec)
    o_ref[...] = rec


def _start_row_gather(src_hbm, dst_ref, index_ref, sem, count):
    def issue(r, carry):
        pltpu.make_async_copy(src_hbm.at[pl.ds(index_ref[r], 1), :], dst_ref.at[pl.ds(r, 1), :], sem).start()
        return carry

    lax.fori_loop(0, count, issue, 0, unroll=8)


def _wait_row_gather(src_hbm, dst_ref, sem, count):
    pltpu.make_async_copy(src_hbm.at[pl.ds(0, count), :], dst_ref, sem).wait()


def _expert_kernel(texp_ref, used_ref, src_ref, x_hbm, g_ref, wg_ref, wu_ref, wd_ref, y_ref, xg_ref, xn_ref, sem,
                   *, tm):
    i = pl.program_id(0)
    j = pl.program_id(1)
    used = used_ref[i] > 0

    @pl.when(jnp.logical_and(j == 0, used))
    def _():
        _start_row_gather(x_hbm, xg_ref, src_ref, sem, tm)
        _wait_row_gather(x_hbm, xg_ref, sem, tm)
        xn_ref[...] = _rms_norm(xg_ref[...], g_ref[...]).astype(BF16)

    @pl.when(jnp.logical_and(j == 0, jnp.logical_not(used)))
    def _():
        y_ref[...] = jnp.zeros_like(y_ref)

    @pl.when(used)
    def _():
        xn = xn_ref[...]
        hidden = _silu(_dot(xn, wg_ref[...])) * _dot(xn, wu_ref[...])
        part = _dot(hidden.astype(BF16), wd_ref[...])

        @pl.when(j == 0)
        def _():
            y_ref[...] = part

        @pl.when(j > 0)
        def _():
            y_ref[...] += part


def _combine_kernel(pos0_ref, pos1_ref, route_ref, x_ref, y_hbm, o_ref, y0_ref, y1_ref, sem, *, tc):
    _start_row_gather(y_hbm, y0_ref, pos0_ref, sem.at[0], tc)
    _start_row_gather(y_hbm, y1_ref, pos1_ref, sem.at[1], tc)
    _wait_row_gather(y_hbm, y0_ref, sem.at[0], tc)
    _wait_row_gather(y_hbm, y1_ref, sem.at[1], tc)
    route = route_ref[...]
    g0 = route[:, ROUTE_GATE_LANE:ROUTE_GATE_LANE + 1]
    g1 = route[:, ROUTE_GATE_LANE + 1:ROUTE_GATE_LANE + 2]
    o_ref[...] = x_ref[...] + (g0 * y0_ref[...] + g1 * y1_ref[...])


def _dispatch_plan(experts, tm, n_tiles):
    n = experts.shape[0]
    flat = experts.reshape(-1)
    onehot = (flat[:, None] == jnp.arange(N_EXPERTS, dtype=jnp.int32)[None, :]).astype(jnp.int32)
    csum = jnp.cumsum(onehot, axis=0)
    rank = jnp.sum((csum - onehot) * onehot, axis=1)
    counts = csum[-1]
    tiles = (counts + tm - 1) // tm
    tile_end = jnp.cumsum(tiles)
    tile_start = tile_end - tiles
    pos = tile_start[flat] * tm + rank
    src = jnp.zeros((n_tiles * tm,), jnp.int32).at[pos].set(jnp.arange(2 * n, dtype=jnp.int32) // 2)
    tile_ids = jnp.arange(n_tiles, dtype=jnp.int32)
    used = (tile_ids < tile_end[-1]).astype(jnp.int32)
    tile_expert = jnp.minimum(jnp.searchsorted(tile_end, tile_ids, side="right"), N_EXPERTS - 1).astype(jnp.int32)
    last_expert = tile_expert[jnp.maximum(tile_end[-1] - 1, 0)]
    tile_expert = jnp.where(used > 0, tile_expert, last_expert)
    pos = pos.reshape(n, 2)
    return src, pos[:, 0], pos[:, 1], tile_expert, used


def _moe_ffn(x, g, r_hi, r_lo, wg, wu, wd, *, tr, tm, tf, tc):
    n, d = x.shape
    e, _, f = wg.shape
    tpe = f // tf
    n_tiles = (TOP_K * n) // tm + e
    smem = pltpu.SMEM

    route = pl.pallas_call(
        _route_kernel,
        grid=(n // tr,),
        in_specs=[
            pl.BlockSpec((tr, d), lambda i: (i, 0)), pl.BlockSpec((1, d), lambda i: (0, 0)),
            pl.BlockSpec((d, LANES), lambda i: (0, 0)), pl.BlockSpec((d, LANES), lambda i: (0, 0)),
        ],
        out_specs=pl.BlockSpec((tr, LANES), lambda i: (i, 0)),
        out_shape=jax.ShapeDtypeStruct((n, LANES), F32),
        compiler_params=pltpu.CompilerParams(dimension_semantics=("arbitrary",), vmem_limit_bytes=VMEM_LIMIT),
        name="moe_route",
    )(x, g, r_hi, r_lo)

    experts = route[:, ROUTE_INDEX_LANE:ROUTE_INDEX_LANE + TOP_K].astype(jnp.int32)
    src, pos0, pos1, tile_expert, used = _dispatch_plan(experts, tm, n_tiles)

    def ff_tile(i, j, texp, used):
        return jnp.where(used[i] > 0, j, tpe - 1)

    y = pl.pallas_call(
        functools.partial(_expert_kernel, tm=tm),
        grid_spec=pltpu.PrefetchScalarGridSpec(
            num_scalar_prefetch=2,
            grid=(n_tiles, tpe),
            in_specs=[
                pl.BlockSpec((tm,), lambda i, j, texp, used: (i,), memory_space=smem),
                pl.BlockSpec(memory_space=pl.ANY),
                pl.BlockSpec((1, d), lambda i, j, texp, used: (0, 0)),
                pl.BlockSpec((None, d, tf), lambda i, j, texp, used: (texp[i], 0, ff_tile(i, j, texp, used))),
                pl.BlockSpec((None, d, tf), lambda i, j, texp, used: (texp[i], 0, ff_tile(i, j, texp, used))),
                pl.BlockSpec((None, tf, d), lambda i, j, texp, used: (texp[i], ff_tile(i, j, texp, used), 0)),
            ],
            out_specs=pl.BlockSpec((tm, d), lambda i, j, texp, used: (i, 0)),
            scratch_shapes=[pltpu.VMEM((tm, d), F32), pltpu.VMEM((tm, d), BF16), pltpu.SemaphoreType.DMA(())],
        ),
        out_shape=jax.ShapeDtypeStruct((n_tiles * tm, d), F32),
        compiler_params=pltpu.CompilerParams(
            dimension_semantics=("arbitrary", "arbitrary"), vmem_limit_bytes=VMEM_LIMIT),
        name="moe_experts",
    )(tile_expert, used, src, x, g, wg, wu, wd)

    return pl.pallas_call(
        functools.partial(_combine_kernel, tc=tc),
        grid=(n // tc,),
        in_specs=[
            pl.BlockSpec((tc,), lambda i: (i,), memory_space=smem),
            pl.BlockSpec((tc,), lambda i: (i,), memory_space=smem),
            pl.BlockSpec((tc, LANES), lambda i: (i, 0)),
            pl.BlockSpec((tc, d), lambda i: (i, 0)),
            pl.BlockSpec(memory_space=pl.ANY),
        ],
        out_specs=pl.BlockSpec((tc, d), lambda i: (i, 0)),
        out_shape=jax.ShapeDtypeStruct((n, d), F32),
        scratch_shapes=[pltpu.VMEM((tc, d), F32), pltpu.VMEM((tc, d), F32), pltpu.SemaphoreType.DMA((2,))],
        compiler_params=pltpu.CompilerParams(dimension_semantics=("arbitrary",), vmem_limit_bytes=VMEM_LIMIT),
        name="moe_combine",
    )(pos0, pos1, route, x, y)


def kernel(x_prompt, x_sample, cache_k_swa, cache_v_swa, state_ret, state_ssm_re, state_ssm_im, state_conv,
           norm0_mix, w_in0, ret_gn, swa_qnorm, swa_knorm, swa_sinks, w_out0, norm0_ffn, ffn0_w_gate,
           ffn0_w_up, ffn0_w_down, norm1_mix, w_in1, ssm_a_re, ssm_a_im, ssm_log_dt, ssm_b_re, ssm_b_im,
           ssm_c_re, ssm_c_im, ssm_d, ssm_w_glu, conv_w, conv_b, conv_ln_g, conv_ln_b, w_out1, norm1_ffn,
           moe_router, moe_w_gate, moe_w_up, moe_w_down):
    bp, lp, d = x_prompt.shape
    bs, ls, _ = x_sample.shape
    row = lambda v: v.astype(F32).reshape(1, -1)
    r_pad = jnp.pad(moe_router.astype(F32), ((0, 0), (0, LANES - N_EXPERTS)))
    r_hi = r_pad.astype(BF16)
    p = dict(
        sinks=swa_sinks.astype(F32), norm0_mix=row(norm0_mix), w_in0=w_in0.astype(BF16), w_out0=w_out0.astype(BF16),
        ret_gn=row(ret_gn), qnorm=row(jnp.tile(swa_qnorm, SWA_HEADS)), knorm=row(jnp.tile(swa_knorm, SWA_KV_HEADS)),
        norm1_mix=row(norm1_mix), w_in1=w_in1.astype(BF16), ssm_a_re=ssm_a_re, ssm_a_im=ssm_a_im,
        ssm_log_dt=ssm_log_dt, ssm_b_re=ssm_b_re, ssm_b_im=ssm_b_im, ssm_c_re=ssm_c_re, ssm_c_im=ssm_c_im,
        ssm_d=row(ssm_d), ssm_w_glu=ssm_w_glu.astype(BF16), conv_w=conv_w.astype(F32), conv_b=row(conv_b),
        conv_ln_g=row(conv_ln_g), conv_ln_b=row(conv_ln_b), w_out1=w_out1.astype(BF16),
    )
    ffn0 = (row(norm0_ffn), ffn0_w_gate.astype(BF16), ffn0_w_up.astype(BF16), ffn0_w_down.astype(BF16))
    moe = (row(norm1_ffn), r_hi, (r_pad - r_hi.astype(F32)).astype(BF16),
           moe_w_gate.astype(BF16), moe_w_up.astype(BF16), moe_w_down.astype(BF16))

    pos_p = jnp.arange(lp, dtype=jnp.int32)
    pos_s = PAST_LEN + jnp.arange(ls, dtype=jnp.int32)
    zeros_kv = jnp.zeros((bp, WINDOW, LANES), F32)
    zeros_ret = jnp.zeros((bp, RET_HEADS, RET_DK, RET_DV), F32)
    zeros_ssm = jnp.zeros((bp, SSM_COLS), F32)
    zeros_conv = jnp.zeros((bp, CONV_K - 1, CONV_WIDTH), F32)
    win = cache_k_swa.shape[1]

    h_p, k_p, v_p, ret_p = _even_mixer(x_prompt, pos_p, zeros_kv, zeros_kv, zeros_ret, True, p)
    h_s, k_s, v_s, ret_s = _even_mixer(
        x_sample, pos_s, cache_k_swa.reshape(bs, win, LANES), cache_v_swa.reshape(bs, win, LANES), state_ret,
        False, p)

    h_p = _dense_ffn(h_p.reshape(bp * lp, d), *ffn0, tm=1024, tf=1408).reshape(bp, lp, d)
    h_s = _dense_ffn(h_s.reshape(bs * ls, d), *ffn0, tm=1024, tf=1408).reshape(bs, ls, d)

    h_p, sre_p, sim_p, conv_p = _odd_mixer(h_p, zeros_ssm, zeros_ssm, zeros_conv, p)
    h_s, sre_s, sim_s, conv_s = _odd_mixer(
        h_s, state_ssm_re.reshape(bs, SSM_COLS), state_ssm_im.reshape(bs, SSM_COLS), state_conv, p)

    h_p = _moe_ffn(h_p.reshape(bp * lp, d), *moe, tr=1024, tm=1024, tf=512, tc=512).reshape(bp, lp, d)
    h_s = _moe_ffn(h_s.reshape(bs * ls, d), *moe, tr=1024, tm=256, tf=512, tc=512).reshape(bs, ls, d)

    kv_p = (bp, WINDOW, SWA_KV_HEADS, SWA_DH)
    kv_s = (bs, win, SWA_KV_HEADS, SWA_DH)
    return (h_p, h_s, k_p.reshape(kv_p), v_p.reshape(kv_p), k_s.reshape(kv_s), v_s.reshape(kv_s), ret_p, ret_s,
            sre_p, sim_p, sre_s, sim_s, conv_p, conv_s)
```
